```python
import math
import jax, jax.numpy as jnp
from jax import lax
import numpy as np

D_MODEL = 4096
BATCH = 4
SEQ = 2048
DEPTH = 2
DEC_BATCH = 128
DEC_SEQ = 4
PAST_LEN = 16384
PAGE_SIZE = 128

DIFF_HEADS = 16
DIFF_HD = D_MODEL // (4 * DIFF_HEADS)
DIFF_KV_HEADS = 1
DIFF_GROUP = DIFF_HEADS // DIFF_KV_HEADS
MLA_HEADS = 16
MLA_NOPE = 128
MLA_ROPE = 64
MLA_V = D_MODEL // (2 * MLA_HEADS)
Q_LORA = 3 * D_MODEL // 16
KV_LORA = D_MODEL // 16
MLA_SCALE = (MLA_NOPE + MLA_ROPE) ** -0.5
ROPE_THETA = 10000.0
COL_DQ = DIFF_HEADS * 2 * DIFF_HD
COL_DK = DIFF_KV_HEADS * 2 * DIFF_HD
COL_DV = DIFF_KV_HEADS * 2 * DIFF_HD
COL_CQ = Q_LORA
COL_CKV = KV_LORA
COL_KR = MLA_ROPE
IN_COLS = COL_DQ + COL_DK + COL_DV + COL_CQ + COL_CKV + COL_KR
SPLIT_AT = (COL_DQ, COL_DQ + COL_DK, COL_DQ + COL_DK + COL_DV,
            COL_DQ + COL_DK + COL_DV + COL_CQ, COL_DQ + COL_DK + COL_DV + COL_CQ + COL_CKV)
MIX_WIDTH = DIFF_HEADS * 2 * DIFF_HD + MLA_HEADS * MLA_V
D_FF = ((8 * D_MODEL // 3 + 255) // 256) * 256
CONV_W = 3
REL_BUCKETS = 32
REL_MAX_EXACT = REL_BUCKETS // 2
REL_MAX_DIST = 128
Q_BLOCK = 128
NORM_EPS = 1e-5
NEG_INF = -1e30
DEEPNORM_ALPHA = (2 * DEPTH) ** 0.25
DEEPNORM_BETA = (8 * DEPTH) ** -0.25

kernel_name = "hymba_diffattn_mla_convglu_deepnorm_step"


def _rmsnorm(x, g):
    xf = x.astype(jnp.float32)
    y = xf * lax.rsqrt(jnp.mean(xf * xf, axis=-1, keepdims=True) + NORM_EPS)
    return (y * g.astype(jnp.float32)).astype(x.dtype)


def _layernorm(x, g, b):
    xf = x.astype(jnp.float32)
    mu = jnp.mean(xf, axis=-1, keepdims=True)
    var = jnp.mean(jnp.square(xf - mu), axis=-1, keepdims=True)
    y = (xf - mu) * lax.rsqrt(var + NORM_EPS) * g.astype(jnp.float32) + b.astype(jnp.float32)
    return y.astype(x.dtype)


def _rope(x, pos):
    half = x.shape[-1] // 2
    inv = ROPE_THETA ** (-jnp.arange(half, dtype=jnp.float32) / half)
    ang = pos.astype(jnp.float32)[:, None] * inv
    ang = ang.reshape(ang.shape[:1] + (1,) * (x.ndim - 3) + (half,))
    cos, sin = jnp.cos(ang), jnp.sin(ang)
    xf = x.astype(jnp.float32)
    x1, x2 = xf[..., :half], xf[..., half:]
    return jnp.concatenate([x1 * cos - x2 * sin, x1 * sin + x2 * cos], axis=-1).astype(x.dtype)


def _rel_bias(rel_table, q_pos, k_pos):
    n = jnp.maximum(q_pos[:, None] - k_pos[None, :], 0)
    nf = jnp.maximum(n, 1).astype(jnp.float32)
    large = REL_MAX_EXACT + (jnp.log(nf / REL_MAX_EXACT) / math.log(REL_MAX_DIST / REL_MAX_EXACT)
                             * (REL_BUCKETS - REL_MAX_EXACT)).astype(jnp.int32)
    large = jnp.minimum(large, REL_BUCKETS - 1)
    bucket = jnp.where(n < REL_MAX_EXACT, n, large)
    return jnp.transpose(rel_table[bucket].astype(jnp.float32), (2, 0, 1))


def _sweep_query_blocks(attend, qs, q_pos):
    S = q_pos.shape[0]
    blk = Q_BLOCK if S % Q_BLOCK == 0 else S
    nb = S // blk

    def split(a):
        return a.reshape((a.shape[0], nb, blk) + a.shape[2:]).swapaxes(0, 1)

    out = lax.map(lambda args: attend(args[0], args[1]),
                  (tuple(split(a) for a in qs), q_pos.reshape(nb, blk)))
    out = out.swapaxes(0, 1)
    return out.reshape((out.shape[0], S) + out.shape[3:])


def _diff_attend(q, q_pos, k, v, k_pos, lam, rel_table):
    sq, t = q_pos.shape[0], k_pos.shape[0]
    s = jnp.einsum('bqngjd,bknjd->bngjqk', q, k, preferred_element_type=jnp.float32) * (DIFF_HD ** -0.5)
    bias = _rel_bias(rel_table, q_pos, k_pos).reshape(DIFF_KV_HEADS, DIFF_GROUP, 1, sq, t)
    mask = k_pos[None, :] <= q_pos[:, None]
    p = jax.nn.softmax(jnp.where(mask, s + bias, NEG_INF), axis=-1)
    a = p[:, :, :, 0] - lam * p[:, :, :, 1]
    return jnp.einsum('bngqk,bknv->bqngv', a.astype(v.dtype), v)


def _mla_attend(q_lat, q_rope, q_pos, c_kv, k_rope, k_pos):
    s = (jnp.einsum('bqhc,bkc->bhqk', q_lat, c_kv, preferred_element_type=jnp.float32)
         + jnp.einsum('bqhr,bkr->bhqk', q_rope, k_rope, preferred_element_type=jnp.float32)) * MLA_SCALE
    mask = k_pos[None, :] <= q_pos[:, None]
    p = jax.nn.softmax(jnp.where(mask, s, NEG_INF), axis=-1)
    return jnp.einsum('bhqk,bkc->bqhc', p.astype(c_kv.dtype), c_kv)


def _layer(x, q_pos, past, layer_idx, lp, rel_table):
    (w_in, q_norm, w_q_up, kv_norm, w_kv_up, lam_q1, lam_k1, lam_q2, lam_k2, subln, w_o,
     ln1_g, ln1_b, w_gate, w_up, conv_w, conv_b, w_down, ln2_g, ln2_b) = lp
    B, S, _ = x.shape
    proj = jnp.einsum('bsd,dc->bsc', x, w_in)
    dq, dk, dv, cq, ckv, kr = jnp.split(proj, SPLIT_AT, axis=-1)
    dq = dq.reshape(B, S, DIFF_KV_HEADS, DIFF_GROUP, 2, DIFF_HD)
    dk = dk.reshape(B, S, DIFF_KV_HEADS, 2, DIFF_HD)
    dv = dv.reshape(B, S, DIFF_KV_HEADS, 2 * DIFF_HD)
    c_kv = _rmsnorm(ckv, kv_norm)
    k_r = _rope(kr, q_pos)
    q = jnp.einsum('bsc,che->bshe', _rmsnorm(cq, q_norm), w_q_up)
    q_nope, q_rope = q[..., :MLA_NOPE], _rope(q[..., MLA_NOPE:], q_pos)
    w_uk, w_uv = w_kv_up[..., :MLA_NOPE], w_kv_up[..., MLA_NOPE:]
    q_lat = jnp.einsum('bshd,chd->bshc', q_nope, w_uk)
    new_mla = jnp.concatenate([c_kv, k_r], axis=-1)

    if past is None:
        keys_dk, keys_dv, keys_c, keys_r = dk, dv, c_kv, k_r
        k_pos = q_pos
        conv_prev = jnp.zeros((B, CONV_W - 1, D_FF), x.dtype)
    else:
        past_dk, past_dv, past_mla, conv_prev = past
        keys_dk = jnp.concatenate([past_dk, dk], axis=1)
        keys_dv = jnp.concatenate([past_dv, dv], axis=1)
        keys_c = jnp.concatenate([past_mla[..., :KV_LORA], c_kv], axis=1)
        keys_r = jnp.concatenate([past_mla[..., KV_LORA:], k_r], axis=1)
        k_pos = jnp.arange(past_dk.shape[1] + S, dtype=jnp.int32)

    lam_init = 0.8 - 0.6 * math.exp(-0.3 * layer_idx)
    lam = (jnp.exp(jnp.sum(lam_q1.astype(jnp.float32) * lam_k1.astype(jnp.float32)))
           - jnp.exp(jnp.sum(lam_q2.astype(jnp.float32) * lam_k2.astype(jnp.float32))) + lam_init)
    diff_o = _sweep_query_blocks(
        lambda qs, qp: _diff_attend(qs[0], qp, keys_dk, keys_dv, k_pos, lam, rel_table), (dq,), q_pos)
    diff_o = _rmsnorm(diff_o.reshape(B, S, DIFF_HEADS, 2 * DIFF_HD), subln) * (1.0 - lam_init)
    mla_lat = _sweep_query_blocks(
        lambda qs, qp: _mla_attend(qs[0], qs[1], qp, keys_c, keys_r, k_pos), (q_lat, q_rope), q_pos)
    mla_o = jnp.einsum('bshc,chd->bshd', mla_lat, w_uv)
    mix = jnp.concatenate([diff_o.reshape(B, S, -1), mla_o.reshape(B, S, -1)], axis=-1)
    x = _layernorm(DEEPNORM_ALPHA * x + jnp.einsum('bsm,md->bsd', mix, w_o), ln1_g, ln1_b)

    g_pre = jnp.einsum('bsd,df->bsf', x, w_gate)
    u = jnp.einsum('bsd,df->bsf', x, w_up)
    padded = jnp.concatenate([conv_prev, g_pre], axis=1)
    g = conv_b + sum(conv_w[j] * padded[:, j:j + S] for j in range(CONV_W))
    h = jax.nn.gelu(g, approximate=False) * u
    x = _layernorm(DEEPNORM_ALPHA * x + jnp.einsum('bsf,fd->bsd', h, w_down), ln2_g, ln2_b)
    new_conv = padded[:, S:]
    return x, (dk, dv, new_mla, new_conv)


def setup_inputs(seed: int = 0) -> dict:
    key = jax.random.key(seed)
    ks = jax.random.split(key, 32)
    f32 = jnp.float32

    def nrm(k, shape, scale):
        return jax.random.normal(k, shape, f32) * scale

    n_pages = PAST_LEN // PAGE_SIZE
    n_used = DEC_BATCH * n_pages
    n_pool = n_used + n_used // 4
    page_table = jax.random.permutation(ks[6], n_pool)[:n_used].reshape(DEC_BATCH, n_pages).astype(jnp.int32)
    return {
        "x_prompt": nrm(ks[0], (BATCH, SEQ, D_MODEL), 1.0),
        "x_sample": nrm(ks[1], (DEC_BATCH, DEC_SEQ, D_MODEL), 1.0),
        "cache_diff_k": nrm(ks[2], (DEPTH, n_pool, PAGE_SIZE, DIFF_KV_HEADS, 2, DIFF_HD), 1.0),
        "cache_diff_v": nrm(ks[3], (DEPTH, n_pool, PAGE_SIZE, DIFF_KV_HEADS, 2 * DIFF_HD), 1.0),
        "cache_mla": nrm(ks[4], (DEPTH, n_pool, PAGE_SIZE, KV_LORA + MLA_ROPE), 1.0),
        "state_conv": nrm(ks[5], (DEPTH, DEC_BATCH, CONV_W - 1, D_FF), 1.0),
        "page_table": page_table,
        "w_in": nrm(ks[7], (DEPTH, D_MODEL, IN_COLS), D_MODEL ** -0.5),
        "q_norm": 1.0 + nrm(ks[8], (DEPTH, Q_LORA), 0.01),
        "w_q_up": nrm(ks[9], (DEPTH, Q_LORA, MLA_HEADS, MLA_NOPE + MLA_ROPE), Q_LORA ** -0.5),
        "kv_norm": 1.0 + nrm(ks[10], (DEPTH, KV_LORA), 0.01),
        "w_kv_up": nrm(ks[11], (DEPTH, KV_LORA, MLA_HEADS, MLA_NOPE + MLA_V), KV_LORA ** -0.5),
        "lam_q1": nrm(ks[12], (DEPTH, DIFF_HD), 0.1),
        "lam_k1": nrm(ks[13], (DEPTH, DIFF_HD), 0.1),
        "lam_q2": nrm(ks[14], (DEPTH, DIFF_HD), 0.1),
        "lam_k2": nrm(ks[15], (DEPTH, DIFF_HD), 0.1),
        "subln": 1.0 + nrm(ks[16], (DEPTH, 2 * DIFF_HD), 0.01),
        "w_o": nrm(ks[17], (DEPTH, MIX_WIDTH, D_MODEL), MIX_WIDTH ** -0.5 * DEEPNORM_BETA),
        "ln1_g": 1.0 + nrm(ks[18], (DEPTH, D_MODEL), 0.01),
        "ln1_b": nrm(ks[19], (DEPTH, D_MODEL), 0.01),
        "w_gate": nrm(ks[20], (DEPTH, D_MODEL, D_FF), D_MODEL ** -0.5),
        "w_up": nrm(ks[21], (DEPTH, D_MODEL, D_FF), D_MODEL ** -0.5),
        "conv_w": nrm(ks[22], (DEPTH, CONV_W, D_FF), CONV_W ** -0.5),
        "conv_b": nrm(ks[23], (DEPTH, D_FF), 0.01),
        "w_down": nrm(ks[24], (DEPTH, D_FF, D_MODEL), D_FF ** -0.5 * DEEPNORM_BETA),
        "ln2_g": 1.0 + nrm(ks[25], (DEPTH, D_MODEL), 0.01),
        "ln2_b": nrm(ks[26], (DEPTH, D_MODEL), 0.01),
        "rel_table": nrm(ks[27], (REL_BUCKETS, DIFF_HEADS), 0.5),
    }


def reference(x_prompt, x_sample, cache_diff_k, cache_diff_v, cache_mla, state_conv, page_table,
              w_in, q_norm, w_q_up, kv_norm, w_kv_up, lam_q1, lam_k1, lam_q2, lam_k2, subln, w_o,
              ln1_g, ln1_b, w_gate, w_up, conv_w, conv_b, w_down, ln2_g, ln2_b, rel_table):
    past_len = page_table.shape[1] * cache_diff_k.shape[2]
    q_pos_prompt = jnp.arange(x_prompt.shape[1], dtype=jnp.int32)
    q_pos_sample = past_len + jnp.arange(x_sample.shape[1], dtype=jnp.int32)

    def gather(cache, l):
        rows = cache[l, page_table]
        return rows.reshape((rows.shape[0], past_len) + rows.shape[3:])

    y_prompt, y_sample = x_prompt, x_sample
    st_prompt, st_sample = [], []
    for l in range(DEPTH):
        lp = tuple(w[l] for w in (w_in, q_norm, w_q_up, kv_norm, w_kv_up, lam_q1, lam_k1, lam_q2, lam_k2,
                                  subln, w_o, ln1_g, ln1_b, w_gate, w_up, conv_w, conv_b, w_down, ln2_g, ln2_b))
        y_prompt, sp = _layer(y_prompt, q_pos_prompt, None, l, lp, rel_table)
        past = (gather(cache_diff_k, l), gather(cache_diff_v, l), gather(cache_mla, l), state_conv[l])
        y_sample, ss = _layer(y_sample, q_pos_sample, past, l, lp, rel_table)
        st_prompt.append(sp)
        st_sample.append(ss)

    new_diff_k_prompt = jnp.stack([s[0] for s in st_prompt])
    new_diff_v_prompt = jnp.stack([s[1] for s in st_prompt])
    new_mla_prompt = jnp.stack([s[2] for s in st_prompt])
    new_conv_prompt = jnp.stack([s[3] for s in st_prompt])
    new_diff_k_sample = jnp.stack([s[0] for s in st_sample])
    new_diff_v_sample = jnp.stack([s[1] for s in st_sample])
    new_mla_sample = jnp.stack([s[2] for s in st_sample])
    new_conv_sample = jnp.stack([s[3] for s in st_sample])
    return (y_prompt, y_sample, new_diff_k_prompt, new_diff_v_prompt, new_mla_prompt, new_conv_prompt,
            new_diff_k_sample, new_diff_v_sample, new_mla_sample, new_conv_sample)
```

```python
import functools
import math

import numpy as np
import jax
import jax.numpy as jnp
from jax import lax
from jax.experimental import pallas as pl
from jax.experimental.pallas import tpu as pltpu

F32 = jnp.float32
BF16 = jnp.bfloat16

ROPE_THETA = 10000.0
REL_MAX_DIST = 128
NORM_EPS = 1e-5
NEG_INF = -1e30
M_INIT = -1e37
FAR_DIST = 1 << 20
CONV_W = 3

V7X_VMEM_LIMIT = 56 * 1024 * 1024
LANE = 128


def _cparams(n_axes):
    return pltpu.CompilerParams(dimension_semantics=("arbitrary",) * n_axes,
                                vmem_limit_bytes=V7X_VMEM_LIMIT)


def _pick(n, cands):
    for c in cands:
        if n % c == 0:
            return c
    return n


def _mm_kernel(x_ref, w_ref, o_ref):
    o_ref[...] = jnp.dot(x_ref[...], w_ref[...], preferred_element_type=F32).astype(o_ref.dtype)


def _matmul(x, w, out_dtype, tm, tn):
    m, k = x.shape
    n = w.shape[1]
    tm, tn = _pick(m, (tm, 256, 128)), _pick(n, (tn, 512, 256, 128))
    return pl.pallas_call(
        _mm_kernel,
        grid=(n // tn, m // tm),
        in_specs=[pl.BlockSpec((tm, k), lambda j, i: (i, 0)),
                  pl.BlockSpec((k, tn), lambda j, i: (0, j))],
        out_specs=pl.BlockSpec((tm, tn), lambda j, i: (i, j)),
        out_shape=jax.ShapeDtypeStruct((m, n), out_dtype),
        compiler_params=_cparams(2),
    )(x, w)


def _head_matmul(x, w, grid, x_spec, w_spec, o_spec, out_shape):
    return pl.pallas_call(
        _mm_kernel, grid=grid, in_specs=[x_spec, w_spec], out_specs=o_spec,
        out_shape=out_shape, compiler_params=_cparams(len(grid)),
    )(x, w)


def _bucket_uppers(n_buckets):
    max_exact = n_buckets // 2
    n = np.arange(0, REL_MAX_DIST + 1)
    nf = np.maximum(n, 1).astype(np.float64)
    v = np.log(nf / max_exact) / math.log(REL_MAX_DIST / max_exact) * (n_buckets - max_exact)
    frac = np.abs(v - np.round(v))[(n > max_exact) & (n < REL_MAX_DIST)]
    assert frac.min() > 1e-3, "bucket boundary too close to an integer distance"
    large = np.minimum(max_exact + np.floor(np.maximum(v + 1e-9, 0)).astype(np.int64), n_buckets - 1)
    bucket = np.where(n < max_exact, n, large)
    assert bucket[REL_MAX_DIST] == n_buckets - 1
    return [int(n[bucket <= b].max()) for b in range(n_buckets - 1)]


def _bias_kernel(tab_ref, n_ref, o_ref, *, uppers):
    h = pl.program_id(1)
    n = n_ref[0]
    val = jnp.full(n.shape, tab_ref[len(uppers), h], F32)
    for b in range(len(uppers) - 1, -1, -1):
        val = jnp.where(n <= uppers[b], tab_ref[b, h], val)
    o_ref[0, 0] = jnp.where(n < 0, NEG_INF, val)


def _bias_from_dist(rel_table, dist):
    g, r, c = dist.shape
    n_buckets, n_heads = rel_table.shape
    return pl.pallas_call(
        functools.partial(_bias_kernel, uppers=_bucket_uppers(n_buckets)),
        grid=(g, n_heads),
        in_specs=[pl.BlockSpec(memory_space=pltpu.SMEM),
                  pl.BlockSpec((1, r, c), lambda i, h: (i, 0, 0))],
        out_specs=pl.BlockSpec((1, 1, r, c), lambda i, h: (i, h, 0, 0)),
        out_shape=jax.ShapeDtypeStruct((g, n_heads, r, c), F32),
        compiler_params=_cparams(2),
    )(rel_table.astype(F32), dist)


_NT = (((1,), (1,)), ((), ()))


def _dot_nt(a, b):
    return lax.dot_general(a, b, _NT, preferred_element_type=F32)


def _online_update(s, v, m_ref, l_ref, acc_ref, idx=None):
    at = (lambda r: r) if idx is None else (lambda r: r.at[idx])
    m_r, l_r, a_r = at(m_ref), at(l_ref), at(acc_ref)
    m_old = m_r[...]
    m_new = jnp.maximum(m_old, jnp.max(s, axis=-1, keepdims=True))
    alpha = jnp.exp(m_old - m_new)
    p = jnp.exp(s - m_new)
    l_r[...] = alpha * l_r[...] + jnp.sum(p, axis=-1, keepdims=True)
    a_r[...] = alpha * a_r[...] + jnp.dot(p.astype(BF16), v, preferred_element_type=F32)
    m_r[...] = m_new


def _init_state(m_ref, l_ref, acc_ref):
    m_ref[...] = jnp.full(m_ref.shape, M_INIT, F32)
    l_ref[...] = jnp.zeros(l_ref.shape, F32)
    acc_ref[...] = jnp.zeros(acc_ref.shape, F32)


def _half_masks(shape, half):
    lane = lax.broadcasted_iota(jnp.int32, shape, len(shape) - 1)
    return lane < half, lane >= half


def _diff_prefill_kernel(lam_ref, q_ref, k_ref, v_ref, bias_ref, o_ref, m_ref, l_ref, acc_ref):
    qi, ki = pl.program_id(1), pl.program_id(2)
    n_heads, t, d2 = q_ref.shape[1:]

    @pl.when(ki == 0)
    def _():
        _init_state(m_ref, l_ref, acc_ref)

    @pl.when(ki <= qi)
    def _():
        k = k_ref[0]
        v = v_ref[0]
        lo, hi = _half_masks((t, d2), d2 // 2)

        def body(h, carry):
            q = q_ref[0, h]
            b = bias_ref[0, h]
            zero = jnp.zeros_like(q)
            _online_update(_dot_nt(jnp.where(lo, q, zero), k) + b, v, m_ref, l_ref, acc_ref, 2 * h)
            _online_update(_dot_nt(jnp.where(hi, q, zero), k) + b, v, m_ref, l_ref, acc_ref, 2 * h + 1)
            return carry

        lax.fori_loop(0, n_heads, body, 0)

    @pl.when(ki == qi)
    def _():
        lam = lam_ref[0, 0]

        def body(h, carry):
            o_ref[0, h] = acc_ref[2 * h] / l_ref[2 * h] - lam * (acc_ref[2 * h + 1] / l_ref[2 * h + 1])
            return carry

        lax.fori_loop(0, n_heads, body, 0)


def _diff_prefill(q, k, v, bias, lam, t):
    b, h, s, d2 = q.shape
    nq = s // t
    kv_spec = pl.BlockSpec((1, t, d2), lambda bi, qi, ki: (bi, jnp.minimum(ki, qi), 0))
    return pl.pallas_call(
        _diff_prefill_kernel,
        grid=(b, nq, nq),
        in_specs=[pl.BlockSpec(memory_space=pltpu.SMEM),
                  pl.BlockSpec((1, h, t, d2), lambda bi, qi, ki: (bi, 0, qi, 0)),
                  kv_spec, kv_spec,
                  pl.BlockSpec((1, h, t, t), lambda bi, qi, ki: (jnp.clip(qi - ki, 0, 2), 0, 0, 0))],
        out_specs=pl.BlockSpec((1, h, t, d2), lambda bi, qi, ki: (bi, 0, qi, 0)),
        out_shape=jax.ShapeDtypeStruct((b, h, s, d2), F32),
        scratch_shapes=[pltpu.VMEM((2 * h, t, 1), F32), pltpu.VMEM((2 * h, t, 1), F32),
                        pltpu.VMEM((2 * h, t, d2), F32)],
        compiler_params=_cparams(3),
    )(lam, q, k, v, bias)


def _mla_prefill_kernel(ql_ref, qr_ref, c_ref, r_ref, o_ref, m_ref, l_ref, acc_ref, *, scale):
    qi, ki = pl.program_id(1), pl.program_id(2)
    n_heads, t, _ = ql_ref.shape[1:]

    @pl.when(ki == 0)
    def _():
        _init_state(m_ref, l_ref, acc_ref)

    @pl.when(ki <= qi)
    def _():
        c = c_ref[0]
        r = r_ref[0]
        row = lax.broadcasted_iota(jnp.int32, (t, t), 0)
        col = lax.broadcasted_iota(jnp.int32, (t, t), 1)
        hidden = jnp.logical_and(col > row, ki == qi)

        def body(h, carry):
            s = (_dot_nt(ql_ref[0, h], c) + _dot_nt(qr_ref[0, h], r)) * scale
            _online_update(jnp.where(hidden, NEG_INF, s), c, m_ref, l_ref, acc_ref, h)
            return carry

        lax.fori_loop(0, n_heads, body, 0)

    @pl.when(ki == qi)
    def _():
        def body(h, carry):
            o_ref[0, h] = (acc_ref[h] / l_ref[h]).astype(o_ref.dtype)
            return carry

        lax.fori_loop(0, n_heads, body, 0)


def _mla_prefill(ql, qr, c, r, scale, t):
    b, h, s, dc = ql.shape
    dr = qr.shape[-1]
    nq = s // t
    q_map = lambda bi, qi, ki: (bi, 0, qi, 0)
    k_map = lambda bi, qi, ki: (bi, jnp.minimum(ki, qi), 0)
    return pl.pallas_call(
        functools.partial(_mla_prefill_kernel, scale=scale),
        grid=(b, nq, nq),
        in_specs=[pl.BlockSpec((1, h, t, dc), q_map), pl.BlockSpec((1, h, t, dr), q_map),
                  pl.BlockSpec((1, t, dc), k_map), pl.BlockSpec((1, t, dr), k_map)],
        out_specs=pl.BlockSpec((1, h, t, dc), q_map),
        out_shape=jax.ShapeDtypeStruct((b, h, s, dc), BF16),
        scratch_shapes=[pltpu.VMEM((h, t, 1), F32), pltpu.VMEM((h, t, 1), F32),
                        pltpu.VMEM((h, t, dc), F32)],
        compiler_params=_cparams(3),
    )(ql, qr, c, r)


PAGES_PER_STEP = 16


def _diff_decode_kernel(pt_ref, lam_ref, q_ref, *refs, n_pages):
    k_refs, v_refs = refs[:n_pages], refs[n_pages:2 * n_pages]
    (knew_ref, vnew_ref, bias_col_ref, bias_tail_ref, bias_new_ref,
     o_ref, m_ref, l_ref, acc_ref) = refs[2 * n_pages:]
    c, nc = pl.program_id(1), pl.num_programs(1)
    q = q_ref[0]

    @pl.when(c == 0)
    def _():
        _init_state(m_ref, l_ref, acc_ref)

    k = jnp.concatenate([r[...] for r in k_refs], axis=0).astype(BF16)
    v = jnp.concatenate([r[...] for r in v_refs], axis=0).astype(BF16)
    s = _dot_nt(q, k)

    @pl.when(c < nc - 1)
    def _():
        _online_update(s + bias_col_ref[:, 0:1], v, m_ref, l_ref, acc_ref)

    @pl.when(c == nc - 1)
    def _():
        _online_update(s + bias_tail_ref[...], v, m_ref, l_ref, acc_ref)
        _online_update(_dot_nt(q, knew_ref[0]) + bias_new_ref[...], vnew_ref[0], m_ref, l_ref, acc_ref)
        o = acc_ref[...] / l_ref[...]
        half = o.shape[0] // 2
        o_ref[0] = o[:half] - lam_ref[0, 0] * o[half:]


def _diff_decode(page_table, lam, q2, cache_k, cache_v, layer, knew, vnew, bias_col, bias_tail, bias_new):
    b, r2, d2 = q2.shape
    n_tab = page_table.shape[1]
    page = cache_k.shape[2]
    p = PAGES_PER_STEP
    assert n_tab % p == 0
    const = lambda bi, ci, pt: (0, 0)
    seq = lambda bi, ci, pt: (bi, 0, 0)

    def page_spec(j):
        return pl.BlockSpec((None, None, page, d2), lambda bi, ci, pt: (layer, pt[bi, ci * p + j], 0, 0))

    grid_spec = pltpu.PrefetchScalarGridSpec(
        num_scalar_prefetch=1,
        grid=(b, n_tab // p),
        in_specs=([pl.BlockSpec(memory_space=pltpu.SMEM), pl.BlockSpec((1, r2, d2), seq)]
                  + [page_spec(j) for j in range(p)] + [page_spec(j) for j in range(p)]
                  + [pl.BlockSpec((1, page, d2), seq), pl.BlockSpec((1, page, d2), seq),
                     pl.BlockSpec((r2, LANE), const), pl.BlockSpec((r2, p * page), const),
                     pl.BlockSpec((r2, page), const)]),
        out_specs=pl.BlockSpec((1, r2 // 2, d2), seq),
        scratch_shapes=[pltpu.VMEM((r2, 1), F32), pltpu.VMEM((r2, 1), F32), pltpu.VMEM((r2, d2), F32)],
    )
    return pl.pallas_call(
        functools.partial(_diff_decode_kernel, n_pages=p),
        grid_spec=grid_spec,
        out_shape=jax.ShapeDtypeStruct((b, r2 // 2, d2), F32),
        compiler_params=_cparams(2),
    )(page_table, lam, q2, *([cache_k] * p), *([cache_v] * p), knew, vnew, bias_col, bias_tail, bias_new)


def _mla_decode_kernel(pt_ref, ql_ref, qr_ref, *refs, n_pages, scale, dc):
    page_refs = refs[:n_pages]
    new_ref, mask_new_ref, o_ref, m_ref, l_ref, acc_ref = refs[n_pages:]
    c, nc = pl.program_id(1), pl.num_programs(1)
    ql, qr = ql_ref[0], qr_ref[0]

    @pl.when(c == 0)
    def _():
        _init_state(m_ref, l_ref, acc_ref)

    kv = jnp.concatenate([r[...] for r in page_refs], axis=0).astype(BF16)
    lat = kv[:, :dc]
    s = (_dot_nt(ql, lat) + _dot_nt(qr, kv[:, dc:])) * scale
    _online_update(s, lat, m_ref, l_ref, acc_ref)

    @pl.when(c == nc - 1)
    def _():
        new = new_ref[0]
        lat_new = new[:, :dc]
        s_new = (_dot_nt(ql, lat_new) + _dot_nt(qr, new[:, dc:])) * scale + mask_new_ref[...]
        _online_update(s_new, lat_new, m_ref, l_ref, acc_ref)
        o_ref[0] = (acc_ref[...] / l_ref[...]).astype(o_ref.dtype)


def _mla_decode(page_table, ql, qr, cache, layer, new, mask_new, scale):
    b, r, dc = ql.shape
    dr = qr.shape[-1]
    n_tab = page_table.shape[1]
    page, width = cache.shape[2:]
    p = PAGES_PER_STEP
    assert n_tab % p == 0 and width == dc + dr
    const = lambda bi, ci, pt: (0, 0)
    seq = lambda bi, ci, pt: (bi, 0, 0)

    def page_spec(j):
        return pl.BlockSpec((None, None, page, width), lambda bi, ci, pt: (layer, pt[bi, ci * p + j], 0, 0))

    grid_spec = pltpu.PrefetchScalarGridSpec(
        num_scalar_prefetch=1,
        grid=(b, n_tab // p),
        in_specs=([pl.BlockSpec((1, r, dc), seq), pl.BlockSpec((1, r, dr), seq)]
                  + [page_spec(j) for j in range(p)]
                  + [pl.BlockSpec((1, page, width), seq), pl.BlockSpec((r, page), const)]),
        out_specs=pl.BlockSpec((1, r, dc), seq),
        scratch_shapes=[pltpu.VMEM((r, 1), F32), pltpu.VMEM((r, 1), F32), pltpu.VMEM((r, dc), F32)],
    )
    return pl.pallas_call(
        functools.partial(_mla_decode_kernel, n_pages=p, scale=scale, dc=dc),
        grid_spec=grid_spec,
        out_shape=jax.ShapeDtypeStruct((b, r, dc), BF16),
        compiler_params=_cparams(2),
    )(page_table, ql, qr, *([cache] * p), new, mask_new)


def _rmsnorm(x, g):
    return x * lax.rsqrt(jnp.mean(x * x, axis=-1, keepdims=True) + NORM_EPS) * g


def _layernorm(x, g, b):
    mu = jnp.mean(x, axis=-1, keepdims=True)
    var = jnp.mean(jnp.square(x - mu), axis=-1, keepdims=True)
    return (x - mu) * lax.rsqrt(var + NORM_EPS) * g + b


def _rope(x, pos):
    half = x.shape[-1] // 2
    inv = ROPE_THETA ** (-jnp.arange(half, dtype=F32) / half)
    ang = pos.astype(F32)[:, None] * inv
    ang = ang.reshape(ang.shape[:1] + (1,) * (x.ndim - 3) + (half,))
    cos, sin = jnp.cos(ang), jnp.sin(ang)
    x1, x2 = x[..., :half], x[..., half:]
    return jnp.concatenate([x1 * cos - x2 * sin, x1 * sin + x2 * cos], axis=-1)


def _layer(x, pos, past, layer, w, bias, dims):
    b, s, d = x.shape
    n = b * s
    h_diff, d2, h_mla, d_nope, d_rope, dc, d_v, q_lora, d_ff = dims
    alpha, mla_scale = w["alpha"], w["mla_scale"]
    tm = _pick(n, (512, 256, 128))

    xb = x.reshape(n, d).astype(BF16)
    proj = _matmul(xb, w["w_in"], F32, tm, _pick(w["w_in"].shape[1], (1152, 384, 128)))
    o = 0
    dq = proj[:, o:o + h_diff * d2]; o += h_diff * d2
    dk = proj[:, o:o + d2]; o += d2
    dv = proj[:, o:o + d2]; o += d2
    cq = proj[:, o:o + q_lora]; o += q_lora
    ckv = proj[:, o:o + dc]; o += dc
    kr = proj[:, o:o + d_rope]

    c_kv = _rmsnorm(ckv, w["kv_norm"])
    k_r = _rope(kr.reshape(b, s, d_rope), pos).reshape(n, d_rope)
    new_mla = jnp.concatenate([c_kv, k_r], axis=-1)

    q = _matmul(_rmsnorm(cq, w["q_norm"]).astype(BF16), w["w_q_up"], F32, tm, 1024)
    q_nope = q[:, :h_mla * d_nope].astype(BF16)
    q_rope = _rope(q[:, h_mla * d_nope:].reshape(b, s, h_mla, d_rope), pos).astype(BF16)
    dq_b = (dq * (d2 // 2) ** -0.5).astype(BF16)
    new_mla_b = new_mla.astype(BF16)
    lam = w["lam"]

    if past is None:
        t = 256
        q_lat = _head_matmul(
            q_nope.reshape(b, s, h_mla * d_nope), w["w_uk_t"], (b, h_mla, s // tm),
            pl.BlockSpec((None, tm, d_nope), lambda bi, hi, i: (bi, i, hi)),
            pl.BlockSpec((None, d_nope, dc), lambda bi, hi, i: (hi, 0, 0)),
            pl.BlockSpec((None, None, tm, dc), lambda bi, hi, i: (bi, hi, i, 0)),
            jax.ShapeDtypeStruct((b, h_mla, s, dc), BF16))
        diff_o = _diff_prefill(dq_b.reshape(b, s, h_diff, d2).transpose(0, 2, 1, 3),
                               dk.astype(BF16).reshape(b, s, d2), dv.astype(BF16).reshape(b, s, d2),
                               bias["prefill"], lam, t)
        diff_o = diff_o.transpose(0, 2, 1, 3)
        mla_lat = _mla_prefill(q_lat, q_rope.transpose(0, 2, 1, 3),
                               new_mla_b[:, :dc].reshape(b, s, dc), new_mla_b[:, dc:].reshape(b, s, d_rope),
                               mla_scale, t)
        mla_o = _head_matmul(
            mla_lat, w["w_uv"], (b, h_mla, s // tm),
            pl.BlockSpec((None, None, tm, dc), lambda bi, hi, i: (bi, hi, i, 0)),
            pl.BlockSpec((None, dc, d_v), lambda bi, hi, i: (hi, 0, 0)),
            pl.BlockSpec((None, tm, d_v), lambda bi, hi, i: (bi, i, hi)),
            jax.ShapeDtypeStruct((b, s, h_mla * d_v), F32)).reshape(n, h_mla * d_v)
        conv_prev = jnp.zeros((b, CONV_W - 1, d_ff), F32)
    else:
        page_table, cache_k, cache_v, cache_mla, conv_prev = past
        page = cache_k.shape[2]
        q_lat = _head_matmul(
            q_nope, w["w_uk_t"], (h_mla, n // tm),
            pl.BlockSpec((tm, d_nope), lambda hi, i: (i, hi)),
            pl.BlockSpec((None, d_nope, dc), lambda hi, i: (hi, 0, 0)),
            pl.BlockSpec((tm, dc), lambda hi, i: (i, hi)),
            jax.ShapeDtypeStruct((n, h_mla * dc), BF16))
        rows = s * h_diff
        qd = dq_b.reshape(b, rows, d2)
        lo, hi = _half_masks(qd.shape, d2 // 2)
        q2 = jnp.concatenate([jnp.where(lo, qd, 0), jnp.where(hi, qd, 0)], axis=1)

        def pad_new(a):
            a = a.reshape(b, s, a.shape[-1])
            return jnp.pad(a, ((0, 0), (0, page - s), (0, 0)))

        diff_o = _diff_decode(page_table, lam, q2, cache_k, cache_v, layer,
                              pad_new(dk.astype(BF16)), pad_new(dv.astype(BF16)),
                              bias["dec_col"], bias["dec_tail"], bias["dec_new"])
        diff_o = diff_o.reshape(b, s, h_diff, d2)
        mla_lat = _mla_decode(page_table, q_lat.reshape(b, s * h_mla, dc),
                              q_rope.reshape(b, s * h_mla, d_rope), cache_mla, layer,
                              pad_new(new_mla_b), bias["dec_mask"], mla_scale)
        mla_o = _head_matmul(
            mla_lat.reshape(n, h_mla * dc), w["w_uv"], (h_mla, n // tm),
            pl.BlockSpec((tm, dc), lambda hi, i: (i, hi)),
            pl.BlockSpec((None, dc, d_v), lambda hi, i: (hi, 0, 0)),
            pl.BlockSpec((tm, d_v), lambda hi, i: (i, hi)),
            jax.ShapeDtypeStruct((n, h_mla * d_v), F32))

    diff_o = _rmsnorm(diff_o, w["subln"]) * (1.0 - w["lam_init"])
    mix = jnp.concatenate([diff_o.reshape(n, h_diff * d2), mla_o], axis=-1).astype(BF16)
    attn = _matmul(mix, w["w_o"], F32, tm, 1024)
    x1 = _layernorm(alpha * x.reshape(n, d) + attn, w["ln1_g"], w["ln1_b"])

    x1b = x1.astype(BF16)
    tm_ff = _pick(n, (1024, 512))
    g_pre = _matmul(x1b, w["w_gate"], F32, tm_ff, 256).reshape(b, s, d_ff)
    u = _matmul(x1b, w["w_up"], F32, tm_ff, 256).reshape(b, s, d_ff)
    padded = jnp.concatenate([conv_prev, g_pre], axis=1)
    g = w["conv_b"] + sum(w["conv_w"][j] * padded[:, j:j + s] for j in range(CONV_W))
    hid = (jax.nn.gelu(g, approximate=False) * u).astype(BF16).reshape(n, d_ff)
    down = _matmul(hid, w["w_down"], F32, _pick(n, (256,)), 512)
    x2 = _layernorm(alpha * x1 + down, w["ln2_g"], w["ln2_b"])
    new_conv = padded[:, s:]
    return x2.reshape(b, s, d), (dk, dv, new_mla, new_conv)


def kernel(x_prompt, x_sample, cache_diff_k, cache_diff_v, cache_mla, state_conv, page_table, w_in, q_norm, w_q_up, kv_norm, w_kv_up, lam_q1, lam_k1, lam_q2, lam_k2, subln, w_o, ln1_g, ln1_b, w_gate, w_up, conv_w, conv_b, w_down, ln2_g, ln2_b, rel_table):
    depth, d_model, in_cols = w_in.shape
    b_p, s_p, _ = x_prompt.shape
    b_s, s_s, _ = x_sample.shape
    n_pool, page = cache_diff_k.shape[1:3]
    d2 = cache_diff_v.shape[-1]
    q_lora, h_mla, qk_dim = w_q_up.shape[1:]
    dc = w_kv_up.shape[1]
    d_rope = cache_mla.shape[-1] - dc
    d_nope = qk_dim - d_rope
    d_v = w_kv_up.shape[-1] - d_nope
    h_diff = rel_table.shape[1]
    d_ff = w_gate.shape[-1]
    dims = (h_diff, d2, h_mla, d_nope, d_rope, dc, d_v, q_lora, d_ff)
    past_len = page_table.shape[1] * page
    assert page >= REL_MAX_DIST and page % LANE == 0 and s_s <= 8

    pos_p = jnp.arange(s_p, dtype=jnp.int32)
    pos_s = past_len + jnp.arange(s_s, dtype=jnp.int32)

    t = 256
    rr = jnp.arange(t, dtype=jnp.int32)[:, None]
    cc = jnp.arange(t, dtype=jnp.int32)[None, :]
    dist_p = jnp.stack([rr - cc, t + rr - cc, jnp.full((t, t), FAR_DIST, jnp.int32)])
    tt = jnp.arange(8, dtype=jnp.int32)[:, None]
    jj = jnp.arange(page, dtype=jnp.int32)[None, :]
    new_ok = jnp.logical_and(jj <= tt, jj < s_s)
    dist_s = jnp.stack([page + tt - jj, jnp.where(new_ok, tt - jj, -1),
                        jnp.full((8, page), FAR_DIST, jnp.int32)])
    bias_p = _bias_from_dist(rel_table, dist_p)
    bias_s = _bias_from_dist(rel_table, dist_s)[:, :, :s_s]
    bias_s = bias_s.transpose(0, 2, 1, 3).reshape(3, s_s * h_diff, page)
    bias_s = jnp.concatenate([bias_s, bias_s], axis=1)
    p = PAGES_PER_STEP
    bias = {
        "prefill": bias_p,
        "dec_col": bias_s[2, :, :LANE],
        "dec_tail": jnp.concatenate([jnp.tile(bias_s[2], (1, p - 1)), bias_s[0]], axis=1),
        "dec_new": bias_s[1],
        "dec_mask": jnp.where(jnp.repeat(new_ok[:s_s], h_mla, axis=0), 0.0, NEG_INF).astype(F32),
    }

    ck = cache_diff_k.reshape(depth, n_pool, page, d2)
    cv = cache_diff_v.reshape(depth, n_pool, page, d2)

    y_p, y_s = x_prompt, x_sample
    st_p, st_s = [], []
    for l in range(depth):
        lam_init = 0.8 - 0.6 * math.exp(-0.3 * l)
        lam = (jnp.exp(jnp.sum(lam_q1[l] * lam_k1[l])) - jnp.exp(jnp.sum(lam_q2[l] * lam_k2[l])) + lam_init)
        pad_in = (-in_cols) % LANE
        wq = w_q_up[l]
        w = {
            "alpha": (2 * depth) ** 0.25,
            "mla_scale": float(qk_dim) ** -0.5,
            "lam": lam.reshape(1, 1).astype(F32),
            "lam_init": lam_init,
            "w_in": jnp.pad(w_in[l], ((0, 0), (0, pad_in))).astype(BF16),
            "w_q_up": jnp.concatenate([wq[:, :, :d_nope].reshape(q_lora, -1),
                                       wq[:, :, d_nope:].reshape(q_lora, -1)], axis=1).astype(BF16),
            "w_uk_t": w_kv_up[l][:, :, :d_nope].transpose(1, 2, 0).astype(BF16),
            "w_uv": w_kv_up[l][:, :, d_nope:].transpose(1, 0, 2).astype(BF16),
            "w_o": w_o[l].astype(BF16), "w_gate": w_gate[l].astype(BF16),
            "w_up": w_up[l].astype(BF16), "w_down": w_down[l].astype(BF16),
            "q_norm": q_norm[l], "kv_norm": kv_norm[l], "subln": subln[l],
            "ln1_g": ln1_g[l], "ln1_b": ln1_b[l], "ln2_g": ln2_g[l], "ln2_b": ln2_b[l],
            "conv_w": conv_w[l], "conv_b": conv_b[l],
        }
        y_p, sp = _layer(y_p, pos_p, None, l, w, bias, dims)
        y_s, ss = _layer(y_s, pos_s, (page_table, ck, cv, cache_mla, state_conv[l]), l, w, bias, dims)
        st_p.append(sp)
        st_s.append(ss)

    def stack(states, i, shape):
        return jnp.stack([st[i].reshape(shape) for st in states])

    outs = [y_p, y_s]
    for states, (bb, ss_) in ((st_p, (b_p, s_p)), (st_s, (b_s, s_s))):
        outs += [stack(states, 0, (bb, ss_, 1, 2, d2 // 2)), stack(states, 1, (bb, ss_, 1, d2)),
                 stack(states, 2, (bb, ss_, dc + d_rope)), stack(states, 3, (bb, CONV_W - 1, d_ff))]
    return tuple(outs)
```

```python
import functools
import math

import numpy as np
import jax
import jax.numpy as jnp
from jax import lax
from jax.experimental import pallas as pl
from jax.experimental.pallas import tpu as pltpu

F32 = jnp.float32
BF16 = jnp.bfloat16

ROPE_THETA = 10000.0
REL_MAX_DIST = 128
NORM_EPS = 1e-5
NEG_INF = -1e30
M_INIT = -1e37
CONV_W = 3

V7X_VMEM_LIMIT = 56 * 1024 * 1024
LANE = 128
SUBLANE = 8
ATTN_TILE = 256
HEAD_GROUP = 4
PAGES_PER_STEP = 16
FF_TILE = 512


def _cparams(n_axes):
    return pltpu.CompilerParams(dimension_semantics=("arbitrary",) * n_axes,
                                vmem_limit_bytes=V7X_VMEM_LIMIT)


def _pick(n, cands):
    for c in cands:
        if n % c == 0:
            return c
    return n


def _mm_kernel(x_ref, w_ref, o_ref):
    o_ref[...] = jnp.dot(x_ref[...], w_ref[...], preferred_element_type=F32).astype(o_ref.dtype)


def _matmul(x, w, out_dtype, tm, tn):
    m, k = x.shape
    n = w.shape[1]
    tm, tn = _pick(m, (tm, 256, 128)), _pick(n, (tn, 512, 256, 128))
    return pl.pallas_call(
        _mm_kernel,
        grid=(n // tn, m // tm),
        in_specs=[pl.BlockSpec((tm, k), lambda j, i: (i, 0)),
                  pl.BlockSpec((k, tn), lambda j, i: (0, j))],
        out_specs=pl.BlockSpec((tm, tn), lambda j, i: (i, j)),
        out_shape=jax.ShapeDtypeStruct((m, n), out_dtype),
        compiler_params=_cparams(2),
    )(x, w)


def _head_matmul(x, w, grid, x_spec, w_spec, o_spec, out_shape):
    return pl.pallas_call(
        _mm_kernel, grid=grid, in_specs=[x_spec, w_spec], out_specs=o_spec,
        out_shape=out_shape, compiler_params=_cparams(len(grid)),
    )(x, w)


def _ffn_in_kernel(x_ref, wg_ref, wu_ref, cw_ref, cb_ref, prev_ref, h_ref, tail_ref, carry_ref, *,
                   tiles_per_seq):
    i = pl.program_id(1)
    x = x_ref[...]
    g = jnp.dot(x, wg_ref[...], preferred_element_type=F32)
    u = jnp.dot(x, wu_ref[...], preferred_element_type=F32)
    prev = jnp.where(i % tiles_per_seq == 0, prev_ref[0], carry_ref[...])
    row = lax.broadcasted_iota(jnp.int32, g.shape, 0)
    g1 = jnp.where(row == 0, prev[7:8], pltpu.roll(g, 1, 0))
    g2 = jnp.where(row == 0, prev[6:7], jnp.where(row == 1, prev[7:8], pltpu.roll(g, 2, 0)))
    cw = cw_ref[...]
    gc = cb_ref[...] + cw[0:1] * g2 + cw[1:2] * g1 + cw[2:3] * g
    act = 0.5 * gc * (1.0 + lax.erf(gc * np.float32(math.sqrt(0.5))))
    h_ref[...] = (act * u).astype(h_ref.dtype)
    tail = g[g.shape[0] - SUBLANE:]
    carry_ref[...] = tail
    tail_ref[0] = tail


def _ffn_in(x, w_gate, w_up, conv_w, conv_b, prev, seq_len, tm):
    n, d = x.shape
    f = w_gate.shape[1]
    tn = FF_TILE
    assert seq_len % tm == 0 and f % tn == 0 and tm >= SUBLANE and CONV_W == 3
    tps = seq_len // tm
    col = lambda j, i: (0, j)
    return pl.pallas_call(
        functools.partial(_ffn_in_kernel, tiles_per_seq=tps),
        grid=(f // tn, n // tm),
        in_specs=[pl.BlockSpec((tm, d), lambda j, i: (i, 0)),
                  pl.BlockSpec((d, tn), col), pl.BlockSpec((d, tn), col),
                  pl.BlockSpec((CONV_W, tn), col), pl.BlockSpec((1, tn), col),
                  pl.BlockSpec((1, SUBLANE, tn), lambda j, i: (i // tps, 0, j))],
        out_specs=[pl.BlockSpec((tm, tn), lambda j, i: (i, j)),
                   pl.BlockSpec((1, SUBLANE, tn), lambda j, i: (i // tps, 0, j))],
        out_shape=[jax.ShapeDtypeStruct((n, f), BF16),
                   jax.ShapeDtypeStruct((n // seq_len, SUBLANE, f), F32)],
        scratch_shapes=[pltpu.VMEM((SUBLANE, tn), F32)],
        compiler_params=_cparams(2),
    )(x, w_gate, w_up, conv_w, conv_b, prev)


def _bucket_uppers(n_buckets):
    max_exact = n_buckets // 2
    n = np.arange(0, REL_MAX_DIST + 1)
    nf = np.maximum(n, 1).astype(np.float64)
    v = np.log(nf / max_exact) / math.log(REL_MAX_DIST / max_exact) * (n_buckets - max_exact)
    frac = np.abs(v - np.round(v))[(n > max_exact) & (n < REL_MAX_DIST)]
    assert frac.min() > 1e-3, "bucket boundary too close to an integer distance"
    large = np.minimum(max_exact + np.floor(np.maximum(v + 1e-9, 0)).astype(np.int64), n_buckets - 1)
    bucket = np.where(n < max_exact, n, large)
    assert bucket[REL_MAX_DIST] == n_buckets - 1
    return [int(n[bucket <= b].max()) for b in range(n_buckets - 1)]


def _bias_kernel(tab_ref, n_ref, o_ref, *, uppers):
    h = pl.program_id(1)
    n = n_ref[0]
    far = tab_ref[len(uppers), h]
    val = jnp.zeros(n.shape, F32)
    for b in range(len(uppers) - 1, -1, -1):
        val = jnp.where(n <= uppers[b], tab_ref[b, h] - far, val)
    o_ref[0, 0] = jnp.where(n < 0, NEG_INF, val)


def _bias_from_dist(rel_table, dist):
    g, r, c = dist.shape
    n_buckets, n_heads = rel_table.shape
    return pl.pallas_call(
        functools.partial(_bias_kernel, uppers=_bucket_uppers(n_buckets)),
        grid=(g, n_heads),
        in_specs=[pl.BlockSpec(memory_space=pltpu.SMEM),
                  pl.BlockSpec((1, r, c), lambda i, h: (i, 0, 0))],
        out_specs=pl.BlockSpec((1, 1, r, c), lambda i, h: (i, h, 0, 0)),
        out_shape=jax.ShapeDtypeStruct((g, n_heads, r, c), F32),
        compiler_params=_cparams(2),
    )(rel_table.astype(F32), dist)


_NT = (((1,), (1,)), ((), ()))


def _dot(a, b):
    return jnp.dot(a, b, preferred_element_type=F32)


def _dot_nt(a, b):
    return lax.dot_general(a, b, _NT, preferred_element_type=F32)


def _softmax_step(s, axis, pv, m_r, l_r, a_r, scale=None):
    m_old = m_r[...]
    m_new = jnp.maximum(m_old, jnp.max(s, axis=axis, keepdims=True))
    dm, ds = m_old - m_new, s - m_new
    if scale is not None:
        dm, ds = dm * scale, ds * scale
    alpha = jnp.exp(dm)
    p = jnp.exp(ds)
    l_r[...] = alpha * l_r[...] + jnp.sum(p, axis=axis, keepdims=True)
    a_r[...] = alpha * a_r[...] + pv(p.astype(BF16))
    m_r[...] = m_new


def _init_state(*state_refs):
    for m_ref, l_ref, acc_ref in zip(state_refs[0::3], state_refs[1::3], state_refs[2::3]):
        m_ref[...] = jnp.full(m_ref.shape, M_INIT, F32)
        l_ref[...] = jnp.zeros(l_ref.shape, F32)
        acc_ref[...] = jnp.zeros(acc_ref.shape, F32)


def _batched(fn, n):
    return lambda p: jnp.stack([fn(p[i]) for i in range(n)])


def _diff_prefill_kernel(lam_ref, q_ref, k_ref, vt_ref, bias_ref, o_ref, m_ref, l_ref, acc_ref):
    qi, ki = pl.program_id(1), pl.program_id(2)
    n_groups, n_tiles = m_ref.shape[:2]
    group = n_tiles // 2

    @pl.when(ki == 0)
    def _():
        _init_state(m_ref, l_ref, acc_ref)

    def sweep(near):
        k = k_ref[0]
        vt = vt_ref[0]

        def body(g, carry):
            tiles = []
            for u in range(group):
                h = g * group + u
                for j in range(2):
                    s = _dot(k, q_ref[0, h, j])
                    tiles.append(s + bias_ref[0, h] if near else s)
            _softmax_step(jnp.stack(tiles), 1, _batched(lambda p: _dot(vt, p), n_tiles),
                          m_ref.at[g], l_ref.at[g], acc_ref.at[g])
            return carry

        lax.fori_loop(0, n_groups, body, 0)

    pl.when(ki < qi - 1)(lambda: sweep(False))
    pl.when(jnp.logical_and(ki >= qi - 1, ki <= qi))(lambda: sweep(True))

    @pl.when(ki == qi)
    def _():
        lam = lam_ref[0, 0]

        def body(g, carry):
            o = acc_ref[g] / l_ref[g]
            for u in range(group):
                o_ref[0, g * group + u] = o[2 * u] - lam * o[2 * u + 1]
            return carry

        lax.fori_loop(0, n_groups, body, 0)


def _diff_prefill(qt, k, vt, bias, lam):
    b, h, _, d2, s = qt.shape
    t = bias.shape[-1]
    nq = s // t
    g = _pick(h, (HEAD_GROUP,))
    kblk = lambda bi, qi, ki: jnp.minimum(ki, qi)
    return pl.pallas_call(
        _diff_prefill_kernel,
        grid=(b, nq, nq),
        in_specs=[pl.BlockSpec(memory_space=pltpu.SMEM),
                  pl.BlockSpec((1, h, 2, d2, t), lambda bi, qi, ki: (bi, 0, 0, 0, qi)),
                  pl.BlockSpec((1, t, d2), lambda bi, qi, ki: (bi, kblk(bi, qi, ki), 0)),
                  pl.BlockSpec((1, d2, t), lambda bi, qi, ki: (bi, 0, kblk(bi, qi, ki))),
                  pl.BlockSpec((1, h, t, t), lambda bi, qi, ki: (jnp.clip(qi - ki, 0, 1), 0, 0, 0))],
        out_specs=pl.BlockSpec((1, h, d2, t), lambda bi, qi, ki: (bi, 0, 0, qi)),
        out_shape=jax.ShapeDtypeStruct((b, h, d2, s), F32),
        scratch_shapes=[pltpu.VMEM((h // g, 2 * g, 1, t), F32), pltpu.VMEM((h // g, 2 * g, 1, t), F32),
                        pltpu.VMEM((h // g, 2 * g, d2, t), F32)],
        compiler_params=_cparams(3),
    )(lam, qt, k, vt, bias)


def _mla_prefill_kernel(qlt_ref, qrt_ref, c_ref, r_ref, ct_ref, o_ref, m_ref, l_ref, acc_ref, *, scale):
    qi, ki = pl.program_id(1), pl.program_id(2)
    n_groups, group = m_ref.shape[:2]
    t = c_ref.shape[1]

    @pl.when(ki == 0)
    def _():
        _init_state(m_ref, l_ref, acc_ref)

    def sweep(diagonal):
        c, r, ct = c_ref[0], r_ref[0], ct_ref[0]
        if diagonal:
            hidden = (lax.broadcasted_iota(jnp.int32, (t, t), 0) > lax.broadcasted_iota(jnp.int32, (t, t), 1))

        def body(g, carry):
            tiles = []
            for u in range(group):
                h = g * group + u
                s = _dot(c, qlt_ref[0, h]) + _dot(r, qrt_ref[0, h])
                tiles.append(jnp.where(hidden, NEG_INF, s) if diagonal else s)
            _softmax_step(jnp.stack(tiles), 1, _batched(lambda p: _dot(ct, p), group),
                          m_ref.at[g], l_ref.at[g], acc_ref.at[g], scale)
            return carry

        lax.fori_loop(0, n_groups, body, 0)

    pl.when(ki < qi)(lambda: sweep(False))

    @pl.when(ki == qi)
    def _():
        sweep(True)

        def body(g, carry):
            o = (acc_ref[g] / l_ref[g]).astype(o_ref.dtype)
            for u in range(group):
                o_ref[0, g * group + u] = o[u]
            return carry

        lax.fori_loop(0, n_groups, body, 0)


def _mla_prefill(qlt, qrt, c, r, ct, scale, t):
    b, h, dc, s = qlt.shape
    dr = qrt.shape[2]
    nq = s // t
    g = _pick(h, (HEAD_GROUP,))
    q_map = lambda bi, qi, ki: (bi, 0, 0, qi)
    k_map = lambda bi, qi, ki: (bi, jnp.minimum(ki, qi), 0)
    return pl.pallas_call(
        functools.partial(_mla_prefill_kernel, scale=scale),
        grid=(b, nq, nq),
        in_specs=[pl.BlockSpec((1, h, dc, t), q_map), pl.BlockSpec((1, h, dr, t), q_map),
                  pl.BlockSpec((1, t, dc), k_map), pl.BlockSpec((1, t, dr), k_map),
                  pl.BlockSpec((1, dc, t), lambda bi, qi, ki: (bi, 0, jnp.minimum(ki, qi)))],
        out_specs=pl.BlockSpec((1, h, dc, t), q_map),
        out_shape=jax.ShapeDtypeStruct((b, h, dc, s), BF16),
        scratch_shapes=[pltpu.VMEM((h // g, g, 1, t), F32), pltpu.VMEM((h // g, g, 1, t), F32),
                        pltpu.VMEM((h // g, g, dc, t), F32)],
        compiler_params=_cparams(3),
    )(qlt, qrt, c, r, ct)


def _decode_kernel(pt_ref, lam_ref, q2_ref, ql_ref, qr_ref, *refs, n_pages, scale, dc):
    kt_refs, v_refs, mt_refs = refs[:n_pages], refs[n_pages:2 * n_pages], refs[2 * n_pages:3 * n_pages]
    (ktn_ref, vn_ref, mtn_ref, bias_last_ref, bias_new_ref, mask_new_ref, od_ref, om_ref,
     md_ref, ld_ref, ad_ref, mm_ref, lm_ref, am_ref) = refs[3 * n_pages:]
    c, nc = pl.program_id(1), pl.num_programs(1)
    q2, ql, qr = q2_ref[0], ql_ref[0], qr_ref[0]
    diff_state = (md_ref, ld_ref, ad_ref)
    mla_state = (mm_ref, lm_ref, am_ref)

    @pl.when(c == 0)
    def _():
        _init_state(*diff_state, *mla_state)

    kt = jnp.concatenate([r[...].astype(BF16) for r in kt_refs], axis=1)
    v = jnp.concatenate([r[...].astype(BF16) for r in v_refs], axis=0)
    mt = jnp.concatenate([r[...].astype(BF16) for r in mt_refs], axis=1)
    page = bias_last_ref.shape[-1]

    s_d = _dot(q2, kt)
    s_d = jnp.concatenate([s_d[:, :-page], s_d[:, -page:] + bias_last_ref[0]], axis=1)
    _softmax_step(s_d, 1, lambda p: _dot(p, v), *diff_state)
    ct = mt[:dc]
    s_m = _dot(ql, ct) + _dot(qr, mt[dc:])
    _softmax_step(s_m, 1, lambda p: _dot_nt(p, ct), *mla_state, scale)

    @pl.when(c == nc - 1)
    def _():
        vn = vn_ref[0]
        _softmax_step(_dot(q2, ktn_ref[0]) + bias_new_ref[...], 1, lambda p: _dot(p, vn), *diff_state)
        o = ad_ref[...] / ld_ref[...]
        half = o.shape[0] // 2
        od_ref[0] = o[:half] - lam_ref[0, 0] * o[half:]
        mtn = mtn_ref[0]
        ctn = mtn[:dc]
        s_n = _dot(ql, ctn) + _dot(qr, mtn[dc:]) + mask_new_ref[...]
        _softmax_step(s_n, 1, lambda p: _dot_nt(p, ctn), *mla_state, scale)
        om_ref[0] = (am_ref[...] / lm_ref[...]).astype(om_ref.dtype)


def _decode_attention(page_table, lam, q2, ql, qr, cache_kt, cache_v, cache_mt, layer,
                      kt_new, v_new, mt_new, bias_last, bias_new, mask_new, scale):
    b, r2, d2 = q2.shape
    r, dc = ql.shape[1:]
    dr = qr.shape[-1]
    n_tab = page_table.shape[1]
    page = cache_v.shape[2]
    p = PAGES_PER_STEP
    assert n_tab % p == 0
    nc = n_tab // p
    const = lambda bi, ci, pt: (0, 0)
    seq = lambda bi, ci, pt: (bi, 0, 0)

    def page_spec(j, shape):
        return pl.BlockSpec((None, None) + shape, lambda bi, ci, pt: (layer, pt[bi, ci * p + j], 0, 0))

    grid_spec = pltpu.PrefetchScalarGridSpec(
        num_scalar_prefetch=1,
        grid=(b, nc),
        in_specs=([pl.BlockSpec(memory_space=pltpu.SMEM),
                   pl.BlockSpec((1, r2, d2), seq), pl.BlockSpec((1, r, dc), seq), pl.BlockSpec((1, r, dr), seq)]
                  + [page_spec(j, (d2, page)) for j in range(p)]
                  + [page_spec(j, (page, d2)) for j in range(p)]
                  + [page_spec(j, (dc + dr, page)) for j in range(p)]
                  + [pl.BlockSpec((1, d2, page), seq), pl.BlockSpec((1, page, d2), seq),
                     pl.BlockSpec((1, dc + dr, page), seq),
                     pl.BlockSpec((1, r2, page), lambda bi, ci, pt: (jnp.where(ci == nc - 1, 1, 0), 0, 0)),
                     pl.BlockSpec((r2, page), const), pl.BlockSpec((r, page), const)]),
        out_specs=[pl.BlockSpec((1, r2 // 2, d2), seq), pl.BlockSpec((1, r, dc), seq)],
        scratch_shapes=[pltpu.VMEM((r2, 1), F32), pltpu.VMEM((r2, 1), F32), pltpu.VMEM((r2, d2), F32),
                        pltpu.VMEM((r, 1), F32), pltpu.VMEM((r, 1), F32), pltpu.VMEM((r, dc), F32)],
    )
    return pl.pallas_call(
        functools.partial(_decode_kernel, n_pages=p, scale=scale, dc=dc),
        grid_spec=grid_spec,
        out_shape=[jax.ShapeDtypeStruct((b, r2 // 2, d2), F32), jax.ShapeDtypeStruct((b, r, dc), BF16)],
        compiler_params=_cparams(2),
    )(page_table, lam, q2, ql, qr, *([cache_kt] * p), *([cache_v] * p), *([cache_mt] * p),
      kt_new, v_new, mt_new, bias_last, bias_new, mask_new)


def _rmsnorm(x, g):
    return x * lax.rsqrt(jnp.mean(x * x, axis=-1, keepdims=True) + NORM_EPS) * g


def _layernorm(x, g, b):
    mu = jnp.mean(x, axis=-1, keepdims=True)
    var = jnp.mean(jnp.square(x - mu), axis=-1, keepdims=True)
    return (x - mu) * lax.rsqrt(var + NORM_EPS) * g + b


def _rope(x, pos):
    half = x.shape[-1] // 2
    inv = ROPE_THETA ** (-jnp.arange(half, dtype=F32) / half)
    ang = pos.astype(F32)[:, None] * inv
    ang = ang.reshape(ang.shape[:1] + (1,) * (x.ndim - 3) + (half,))
    cos, sin = jnp.cos(ang), jnp.sin(ang)
    x1, x2 = x[..., :half], x[..., half:]
    return jnp.concatenate([x1 * cos - x2 * sin, x1 * sin + x2 * cos], axis=-1)


def _half_masked(q, axis):
    d2 = q.shape[axis]
    idx = lax.broadcasted_iota(jnp.int32, q.shape, axis)
    zero = jnp.zeros_like(q)
    return jnp.stack([jnp.where(idx < d2 // 2, q, zero), jnp.where(idx >= d2 // 2, q, zero)], axis=axis)


def _layer(x, pos, past, layer, w, bias, dims):
    b, s, d = x.shape
    n = b * s
    h_diff, d2, h_mla, d_nope, d_rope, dc, d_v, q_lora, d_ff = dims
    alpha, mla_scale = w["alpha"], w["mla_scale"]
    tm = _pick(n, (512, 256, 128))

    xb = x.reshape(n, d).astype(BF16)
    proj = _matmul(xb, w["w_in"], F32, tm, _pick(w["w_in"].shape[1], (1152, 384, 128)))
    o = 0
    dq = proj[:, o:o + h_diff * d2]; o += h_diff * d2
    dk = proj[:, o:o + d2]; o += d2
    dv = proj[:, o:o + d2]; o += d2
    cq = proj[:, o:o + q_lora]; o += q_lora
    ckv = proj[:, o:o + dc]; o += dc
    kr = proj[:, o:o + d_rope]

    c_kv = _rmsnorm(ckv, w["kv_norm"])
    k_r = _rope(kr.reshape(b, s, d_rope), pos).reshape(n, d_rope)
    new_mla = jnp.concatenate([c_kv, k_r], axis=-1)

    q = _matmul(_rmsnorm(cq, w["q_norm"]).astype(BF16), w["w_q_up"], F32, tm, 1024)
    q_nope = q[:, :h_mla * d_nope].astype(BF16)
    q_rope = _rope(q[:, h_mla * d_nope:].reshape(b, s, h_mla, d_rope), pos).astype(BF16)
    dq_b = (dq * (d2 // 2) ** -0.5).astype(BF16)
    dk_b = dk.astype(BF16).reshape(b, s, d2)
    dv_b = dv.astype(BF16).reshape(b, s, d2)
    new_mla_b = new_mla.astype(BF16).reshape(b, s, dc + d_rope)
    lam = w["lam"]

    if past is None:
        q_lat = _head_matmul(
            q_nope.reshape(b, s, h_mla * d_nope), w["w_uk_t"], (b, h_mla, s // tm),
            pl.BlockSpec((None, tm, d_nope), lambda bi, hi, i: (bi, i, hi)),
            pl.BlockSpec((None, d_nope, dc), lambda bi, hi, i: (hi, 0, 0)),
            pl.BlockSpec((None, None, tm, dc), lambda bi, hi, i: (bi, hi, i, 0)),
            jax.ShapeDtypeStruct((b, h_mla, s, dc), BF16))
        qt = _half_masked(dq_b.reshape(b, s, h_diff, d2).transpose(0, 2, 3, 1), 2)
        diff_o = _diff_prefill(qt, dk_b, dv_b.transpose(0, 2, 1), bias["prefill"], lam)
        diff_o = diff_o.transpose(0, 3, 1, 2)
        mla_lat = _mla_prefill(q_lat.transpose(0, 1, 3, 2), q_rope.transpose(0, 2, 3, 1),
                               new_mla_b[..., :dc], new_mla_b[..., dc:],
                               new_mla_b[..., :dc].transpose(0, 2, 1), mla_scale, ATTN_TILE)
        mla_o = _head_matmul(
            mla_lat.transpose(0, 1, 3, 2), w["w_uv"], (b, h_mla, s // tm),
            pl.BlockSpec((None, None, tm, dc), lambda bi, hi, i: (bi, hi, i, 0)),
            pl.BlockSpec((None, dc, d_v), lambda bi, hi, i: (hi, 0, 0)),
            pl.BlockSpec((None, tm, d_v), lambda bi, hi, i: (bi, i, hi)),
            jax.ShapeDtypeStruct((b, s, h_mla * d_v), F32)).reshape(n, h_mla * d_v)
    else:
        page_table, cache_kt, cache_v, cache_mt, conv_prev = past
        page = cache_v.shape[2]
        q_lat = _head_matmul(
            q_nope, w["w_uk_t"], (h_mla, n // tm),
            pl.BlockSpec((tm, d_nope), lambda hi, i: (i, hi)),
            pl.BlockSpec((None, d_nope, dc), lambda hi, i: (hi, 0, 0)),
            pl.BlockSpec((tm, dc), lambda hi, i: (i, hi)),
            jax.ShapeDtypeStruct((n, h_mla * dc), BF16))
        q2 = _half_masked(dq_b.reshape(b, s * h_diff, d2), 2).transpose(0, 2, 1, 3)
        q2 = q2.reshape(b, 2 * s * h_diff, d2)

        def pad_new(a):
            return jnp.pad(a, ((0, 0), (0, page - s), (0, 0)))

        diff_o, mla_lat = _decode_attention(
            page_table, lam, q2, q_lat.reshape(b, s * h_mla, dc), q_rope.reshape(b, s * h_mla, d_rope),
            cache_kt, cache_v, cache_mt, layer,
            pad_new(dk_b).transpose(0, 2, 1), pad_new(dv_b), pad_new(new_mla_b).transpose(0, 2, 1),
            bias["dec_last"], bias["dec_new"], bias["dec_mask"], mla_scale)
        diff_o = diff_o.reshape(b, s, h_diff, d2)
        mla_o = _head_matmul(
            mla_lat.reshape(n, h_mla * dc), w["w_uv"], (h_mla, n // tm),
            pl.BlockSpec((tm, dc), lambda hi, i: (i, hi)),
            pl.BlockSpec((None, dc, d_v), lambda hi, i: (hi, 0, 0)),
            pl.BlockSpec((tm, d_v), lambda hi, i: (i, hi)),
            jax.ShapeDtypeStruct((n, h_mla * d_v), F32))

    diff_o = _rmsnorm(diff_o, w["subln"]) * (1.0 - w["lam_init"])
    mix = jnp.concatenate([diff_o.reshape(n, h_diff * d2), mla_o], axis=-1).astype(BF16)
    attn = _matmul(mix, w["w_o"], F32, tm, 1024)
    x1 = _layernorm(alpha * x.reshape(n, d) + attn, w["ln1_g"], w["ln1_b"])

    x1b = x1.astype(BF16)
    f_pad = w["w_gate"].shape[1]
    if past is None:
        prev = jnp.zeros((b, SUBLANE, f_pad), F32)
        hid, tail = _ffn_in(x1b, w["w_gate"], w["w_up"], w["conv_w"], w["conv_b"], prev, s, tm)
        new_conv = tail[:, SUBLANE - (CONV_W - 1):, :d_ff]
    else:
        g_pre = _matmul(x1b, w["w_gate"], F32, tm, FF_TILE).reshape(b, s, f_pad)
        u = _matmul(x1b, w["w_up"], F32, tm, FF_TILE).reshape(b, s, f_pad)
        padded = jnp.concatenate([jnp.pad(conv_prev, ((0, 0), (0, 0), (0, f_pad - d_ff))), g_pre], axis=1)
        g = w["conv_b"] + sum(w["conv_w"][j] * padded[:, j:j + s] for j in range(CONV_W))
        hid = (jax.nn.gelu(g, approximate=False) * u).astype(BF16).reshape(n, f_pad)
        new_conv = padded[:, s:, :d_ff]
    down = _matmul(hid, w["w_down"], F32, 256, 512)
    x2 = _layernorm(alpha * x1 + down, w["ln2_g"], w["ln2_b"])
    return x2.reshape(b, s, d), (dk, dv, new_mla, new_conv)


def kernel(x_prompt, x_sample, cache_diff_k, cache_diff_v, cache_mla, state_conv, page_table, w_in, q_norm, w_q_up, kv_norm, w_kv_up, lam_q1, lam_k1, lam_q2, lam_k2, subln, w_o, ln1_g, ln1_b, w_gate, w_up, conv_w, conv_b, w_down, ln2_g, ln2_b, rel_table):
    depth, d_model, in_cols = w_in.shape
    b_p, s_p, _ = x_prompt.shape
    b_s, s_s, _ = x_sample.shape
    n_pool, page = cache_diff_k.shape[1:3]
    d2 = cache_diff_v.shape[-1]
    q_lora, h_mla, qk_dim = w_q_up.shape[1:]
    dc = w_kv_up.shape[1]
    d_rope = cache_mla.shape[-1] - dc
    d_nope = qk_dim - d_rope
    d_v = w_kv_up.shape[-1] - d_nope
    h_diff = rel_table.shape[1]
    d_ff = w_gate.shape[-1]
    dims = (h_diff, d2, h_mla, d_nope, d_rope, dc, d_v, q_lora, d_ff)
    past_len = page_table.shape[1] * page
    t = ATTN_TILE
    assert page >= REL_MAX_DIST and t >= REL_MAX_DIST and page % LANE == 0 and s_s <= SUBLANE
    assert cache_diff_k.shape[3] == 1 and cache_diff_v.shape[3] == 1, "one shared key/value head"

    pos_p = jnp.arange(s_p, dtype=jnp.int32)
    pos_s = past_len + jnp.arange(s_s, dtype=jnp.int32)

    kk = jnp.arange(t, dtype=jnp.int32)[:, None]
    qq = jnp.arange(t, dtype=jnp.int32)[None, :]
    bias_p = _bias_from_dist(rel_table, jnp.stack([qq - kk, t + qq - kk]))
    tt = jnp.arange(SUBLANE, dtype=jnp.int32)[:, None]
    jj = jnp.arange(page, dtype=jnp.int32)[None, :]
    new_ok = jnp.logical_and(jj <= tt, jj < s_s)
    bias_s = _bias_from_dist(rel_table, jnp.stack([page + tt - jj, jnp.where(new_ok, tt - jj, -1)]))
    bias_s = bias_s[:, :, :s_s].transpose(0, 2, 1, 3).reshape(2, s_s * h_diff, page)
    bias_s = jnp.concatenate([bias_s, bias_s], axis=1)
    bias = {
        "prefill": bias_p,
        "dec_last": jnp.stack([jnp.zeros_like(bias_s[0]), bias_s[0]]),
        "dec_new": bias_s[1],
        "dec_mask": jnp.where(jnp.repeat(new_ok[:s_s], h_mla, axis=0), 0.0, NEG_INF).astype(F32),
    }

    cache_kt = cache_diff_k.transpose(0, 1, 3, 4, 5, 2).reshape(depth, n_pool, d2, page)
    cache_v = cache_diff_v.reshape(depth, n_pool, page, d2)
    cache_mt = cache_mla.transpose(0, 1, 3, 2)

    f_pad = -(-d_ff // FF_TILE) * FF_TILE
    pad_f = f_pad - d_ff
    y_p, y_s = x_prompt, x_sample
    st_p, st_s = [], []
    for l in range(depth):
        lam_init = 0.8 - 0.6 * math.exp(-0.3 * l)
        lam = (jnp.exp(jnp.sum(lam_q1[l] * lam_k1[l])) - jnp.exp(jnp.sum(lam_q2[l] * lam_k2[l])) + lam_init)
        pad_in = (-in_cols) % LANE
        wq = w_q_up[l]
        w = {
            "alpha": (2 * depth) ** 0.25,
            "mla_scale": float(qk_dim) ** -0.5,
            "lam": lam.reshape(1, 1).astype(F32),
            "lam_init": lam_init,
            "w_in": jnp.pad(w_in[l], ((0, 0), (0, pad_in))).astype(BF16),
            "w_q_up": jnp.concatenate([wq[:, :, :d_nope].reshape(q_lora, -1),
                                       wq[:, :, d_nope:].reshape(q_lora, -1)], axis=1).astype(BF16),
            "w_uk_t": w_kv_up[l][:, :, :d_nope].transpose(1, 2, 0).astype(BF16),
            "w_uv": w_kv_up[l][:, :, d_nope:].transpose(1, 0, 2).astype(BF16),
            "w_o": w_o[l].astype(BF16),
            "w_gate": jnp.pad(w_gate[l], ((0, 0), (0, pad_f))).astype(BF16),
            "w_up": jnp.pad(w_up[l], ((0, 0), (0, pad_f))).astype(BF16),
            "w_down": jnp.pad(w_down[l], ((0, pad_f), (0, 0))).astype(BF16),
            "conv_w": jnp.pad(conv_w[l], ((0, 0), (0, pad_f))),
            "conv_b": jnp.pad(conv_b[l], (0, pad_f)).reshape(1, f_pad),
            "q_norm": q_norm[l], "kv_norm": kv_norm[l], "subln": subln[l],
            "ln1_g": ln1_g[l], "ln1_b": ln1_b[l], "ln2_g": ln2_g[l], "ln2_b": ln2_b[l],
        }
        y_p, sp = _layer(y_p, pos_p, None, l, w, bias, dims)
        y_s, ss = _layer(y_s, pos_s, (page_table, cache_kt, cache_v, cache_mt, state_conv[l]), l, w, bias, dims)
        st_p.append(sp)
        st_s.append(ss)

    def stack(states, i, shape):
        return jnp.stack([st[i].reshape(shape) for st in states])

    outs = [y_p, y_s]
    for states, (bb, ss_) in ((st_p, (b_p, s_p)), (st_s, (b_s, s_s))):
        outs += [stack(states, 0, (bb, ss_, 1, 2, d2 // 2)), stack(states, 1, (bb, ss_, 1, d2)),
                 stack(states, 2, (bb, ss_, dc + d_rope)), stack(states, 3, (bb, CONV_W - 1, d_ff))]
    return tuple(outs)
```

```python
import functools
import math

import numpy as np
import jax
import jax.numpy as jnp
from jax import lax
from jax.experimental import pallas as pl
from jax.experimental.pallas import tpu as pltpu

F32 = jnp.float32
BF16 = jnp.bfloat16

ROPE_THETA = 10000.0
REL_MAX_DIST = 128
NORM_EPS = 1e-5
NEG_INF = -1e30
M_INIT = -1e37
CONV_W = 3

V7X_VMEM_LIMIT = 56 * 1024 * 1024
LANE = 128
SUBLANE = 8
ATTN_TILE = 256
HEAD_GROUP = 4
PAGES_PER_STEP = 16
FF_TILE = 512


def _cparams(n_axes):
    return pltpu.CompilerParams(dimension_semantics=("arbitrary",) * n_axes,
                                vmem_limit_bytes=V7X_VMEM_LIMIT)


def _pick(n, cands):
    for c in cands:
        if n % c == 0:
            return c
    return n


def _mm_kernel(x_ref, w_ref, o_ref):
    o_ref[...] = jnp.dot(x_ref[...], w_ref[...], preferred_element_type=F32).astype(o_ref.dtype)


def _mm_nt_kernel(x_ref, w_ref, o_ref):
    o_ref[...] = lax.dot_general(x_ref[...], w_ref[...], (((1,), (1,)), ((), ())),
                                 preferred_element_type=F32).astype(o_ref.dtype)


def _matmul(x, w, layer, out_dtype, tm, tn):
    m, k = x.shape
    n = w.shape[2]
    tm, tn = _pick(m, (tm, 256, 128)), _pick(n, (tn, 512, 256, 128))
    return pl.pallas_call(
        _mm_kernel,
        grid=(n // tn, m // tm),
        in_specs=[pl.BlockSpec((tm, k), lambda j, i: (i, 0)),
                  pl.BlockSpec((None, k, tn), lambda j, i: (layer, 0, j))],
        out_specs=pl.BlockSpec((tm, tn), lambda j, i: (i, j)),
        out_shape=jax.ShapeDtypeStruct((m, n), out_dtype),
        compiler_params=_cparams(2),
    )(x, w)


def _head_matmul(x, w, grid, x_spec, w_spec, o_spec, out_shape, body=_mm_kernel):
    return pl.pallas_call(
        body, grid=grid, in_specs=[x_spec, w_spec], out_specs=o_spec,
        out_shape=out_shape, compiler_params=_cparams(len(grid)),
    )(x, w)


def _add_ln_kernel(x_ref, y_ref, g_ref, b_ref, o_ref, ob_ref, *, alpha):
    z = alpha * x_ref[...] + y_ref[...]
    zc = z - jnp.mean(z, axis=-1, keepdims=True)
    var = jnp.mean(zc * zc, axis=-1, keepdims=True)
    o = zc * lax.rsqrt(var + NORM_EPS) * g_ref[...] + b_ref[...]
    o_ref[...] = o
    ob_ref[...] = o.astype(ob_ref.dtype)


def _add_layernorm(x, y, g, b, alpha):
    m, d = x.shape
    tr = _pick(m, (256, 128))
    row = pl.BlockSpec((tr, d), lambda i: (i, 0))
    vec = pl.BlockSpec((1, d), lambda i: (0, 0))
    return pl.pallas_call(
        functools.partial(_add_ln_kernel, alpha=alpha),
        grid=(m // tr,),
        in_specs=[row, row, vec, vec],
        out_specs=[row, row],
        out_shape=[jax.ShapeDtypeStruct((m, d), F32), jax.ShapeDtypeStruct((m, d), BF16)],
        compiler_params=_cparams(1),
    )(x, y, g.reshape(1, d), b.reshape(1, d))


def _ffn_in_kernel(x_ref, wg_ref, wu_ref, cw_ref, cb_ref, prev_ref, h_ref, tail_ref, carry_ref, *,
                   tiles_per_seq):
    i = pl.program_id(1)
    x = x_ref[...]
    g = jnp.dot(x, wg_ref[...], preferred_element_type=F32)
    u = jnp.dot(x, wu_ref[...], preferred_element_type=F32)
    prev = jnp.where(i % tiles_per_seq == 0, prev_ref[0], carry_ref[...])
    row = lax.broadcasted_iota(jnp.int32, g.shape, 0)
    g1 = jnp.where(row == 0, prev[7:8], pltpu.roll(g, 1, 0))
    g2 = jnp.where(row == 0, prev[6:7], jnp.where(row == 1, prev[7:8], pltpu.roll(g, 2, 0)))
    cw = cw_ref[...]
    gc = cb_ref[...] + cw[0:1] * g2 + cw[1:2] * g1 + cw[2:3] * g
    act = 0.5 * gc * (1.0 + lax.erf(gc * np.float32(math.sqrt(0.5))))
    h_ref[...] = (act * u).astype(h_ref.dtype)
    tail = g[g.shape[0] - SUBLANE:]
    carry_ref[...] = tail
    tail_ref[0] = tail


def _ffn_in(x, w_gate, w_up, layer, conv_w, conv_b, prev, seq_len, tm):
    n, d = x.shape
    f = w_gate.shape[2]
    tn = FF_TILE
    assert seq_len % tm == 0 and f % tn == 0 and tm >= SUBLANE and CONV_W == 3
    tps = seq_len // tm
    col = lambda j, i: (0, j)
    wcol = pl.BlockSpec((None, d, tn), lambda j, i: (layer, 0, j))
    return pl.pallas_call(
        functools.partial(_ffn_in_kernel, tiles_per_seq=tps),
        grid=(f // tn, n // tm),
        in_specs=[pl.BlockSpec((tm, d), lambda j, i: (i, 0)), wcol, wcol,
                  pl.BlockSpec((CONV_W, tn), col), pl.BlockSpec((1, tn), col),
                  pl.BlockSpec((1, SUBLANE, tn), lambda j, i: (i // tps, 0, j))],
        out_specs=[pl.BlockSpec((tm, tn), lambda j, i: (i, j)),
                   pl.BlockSpec((1, SUBLANE, tn), lambda j, i: (i // tps, 0, j))],
        out_shape=[jax.ShapeDtypeStruct((n, f), BF16),
                   jax.ShapeDtypeStruct((n // seq_len, SUBLANE, f), F32)],
        scratch_shapes=[pltpu.VMEM((SUBLANE, tn), F32)],
        compiler_params=_cparams(2),
    )(x, w_gate, w_up, conv_w, conv_b, prev)


def _bucket_uppers(n_buckets):
    max_exact = n_buckets // 2
    n = np.arange(0, REL_MAX_DIST + 1)
    nf = np.maximum(n, 1).astype(np.float64)
    v = np.log(nf / max_exact) / math.log(REL_MAX_DIST / max_exact) * (n_buckets - max_exact)
    frac = np.abs(v - np.round(v))[(n > max_exact) & (n < REL_MAX_DIST)]
    assert frac.min() > 1e-3, "bucket boundary too close to an integer distance"
    large = np.minimum(max_exact + np.floor(np.maximum(v + 1e-9, 0)).astype(np.int64), n_buckets - 1)
    bucket = np.where(n < max_exact, n, large)
    assert bucket[REL_MAX_DIST] == n_buckets - 1
    return [int(n[bucket <= b].max()) for b in range(n_buckets - 1)]


def _bias_kernel(tab_ref, n_ref, o_ref, *, uppers):
    h = pl.program_id(1)
    n = n_ref[0]
    far = tab_ref[len(uppers), h]
    val = jnp.zeros(n.shape, F32)
    for b in range(len(uppers) - 1, -1, -1):
        val = jnp.where(n <= uppers[b], tab_ref[b, h] - far, val)
    o_ref[0, 0] = jnp.where(n < 0, NEG_INF, val)


def _bias_from_dist(rel_table, dist):
    g, r, c = dist.shape
    n_buckets, n_heads = rel_table.shape
    return pl.pallas_call(
        functools.partial(_bias_kernel, uppers=_bucket_uppers(n_buckets)),
        grid=(g, n_heads),
        in_specs=[pl.BlockSpec(memory_space=pltpu.SMEM),
                  pl.BlockSpec((1, r, c), lambda i, h: (i, 0, 0))],
        out_specs=pl.BlockSpec((1, 1, r, c), lambda i, h: (i, h, 0, 0)),
        out_shape=jax.ShapeDtypeStruct((g, n_heads, r, c), F32),
        compiler_params=_cparams(2),
    )(rel_table.astype(F32), dist)


_NT = (((1,), (1,)), ((), ()))


def _dot(a, b):
    return jnp.dot(a, b, preferred_element_type=F32)


def _dot_nt(a, b):
    return lax.dot_general(a, b, _NT, preferred_element_type=F32)


def _softmax_step(s, axis, pv, m_r, l_r, a_r, scale=None):
    m_old = m_r[...]
    m_new = jnp.maximum(m_old, jnp.max(s, axis=axis, keepdims=True))
    dm, ds = m_old - m_new, s - m_new
    if scale is not None:
        dm, ds = dm * scale, ds * scale
    alpha = jnp.exp(dm)
    p = jnp.exp(ds)
    l_r[...] = alpha * l_r[...] + jnp.sum(p, axis=axis, keepdims=True)
    a_r[...] = alpha * a_r[...] + pv(p.astype(BF16))
    m_r[...] = m_new


def _init_state(*state_refs):
    for m_ref, l_ref, acc_ref in zip(state_refs[0::3], state_refs[1::3], state_refs[2::3]):
        m_ref[...] = jnp.full(m_ref.shape, M_INIT, F32)
        l_ref[...] = jnp.zeros(l_ref.shape, F32)
        acc_ref[...] = jnp.zeros(acc_ref.shape, F32)


def _batched(fn, n):
    return lambda p: jnp.stack([fn(p[i]) for i in range(n)])


def _diff_prefill_kernel(lam_ref, q_ref, k_ref, vt_ref, bias_ref, o_ref, m_ref, l_ref, acc_ref):
    qi, ki = pl.program_id(1), pl.program_id(2)
    n_groups, n_tiles = m_ref.shape[:2]
    group = n_tiles // 2

    @pl.when(ki == 0)
    def _():
        _init_state(m_ref, l_ref, acc_ref)

    def sweep(near):
        k = k_ref[0]
        vt = vt_ref[0]

        def body(g, carry):
            tiles = []
            for u in range(group):
                h = g * group + u
                for j in range(2):
                    s = _dot(k, q_ref[0, h, j])
                    tiles.append(s + bias_ref[0, h] if near else s)
            _softmax_step(jnp.stack(tiles), 1, _batched(lambda p: _dot(vt, p), n_tiles),
                          m_ref.at[g], l_ref.at[g], acc_ref.at[g])
            return carry

        lax.fori_loop(0, n_groups, body, 0)

    pl.when(ki < qi - 1)(lambda: sweep(False))
    pl.when(jnp.logical_and(ki >= qi - 1, ki <= qi))(lambda: sweep(True))

    @pl.when(ki == qi)
    def _():
        lam = lam_ref[0, 0]

        def body(g, carry):
            o = acc_ref[g] / l_ref[g]
            for u in range(group):
                o_ref[0, g * group + u] = o[2 * u] - lam * o[2 * u + 1]
            return carry

        lax.fori_loop(0, n_groups, body, 0)


def _diff_prefill(qt, k, vt, bias, lam):
    b, h, _, d2, s = qt.shape
    t = bias.shape[-1]
    nq = s // t
    g = _pick(h, (HEAD_GROUP,))
    kblk = lambda bi, qi, ki: jnp.minimum(ki, qi)
    return pl.pallas_call(
        _diff_prefill_kernel,
        grid=(b, nq, nq),
        in_specs=[pl.BlockSpec(memory_space=pltpu.SMEM),
                  pl.BlockSpec((1, h, 2, d2, t), lambda bi, qi, ki: (bi, 0, 0, 0, qi)),
                  pl.BlockSpec((1, t, d2), lambda bi, qi, ki: (bi, kblk(bi, qi, ki), 0)),
                  pl.BlockSpec((1, d2, t), lambda bi, qi, ki: (bi, 0, kblk(bi, qi, ki))),
                  pl.BlockSpec((1, h, t, t), lambda bi, qi, ki: (jnp.clip(qi - ki, 0, 1), 0, 0, 0))],
        out_specs=pl.BlockSpec((1, h, d2, t), lambda bi, qi, ki: (bi, 0, 0, qi)),
        out_shape=jax.ShapeDtypeStruct((b, h, d2, s), F32),
        scratch_shapes=[pltpu.VMEM((h // g, 2 * g, 1, t), F32), pltpu.VMEM((h // g, 2 * g, 1, t), F32),
                        pltpu.VMEM((h // g, 2 * g, d2, t), F32)],
        compiler_params=_cparams(3),
    )(lam, qt, k, vt, bias)


def _mla_prefill_kernel(qlt_ref, qrt_ref, c_ref, r_ref, ct_ref, o_ref, m_ref, l_ref, acc_ref, *, scale):
    qi, ki = pl.program_id(1), pl.program_id(2)
    n_groups, group = m_ref.shape[:2]
    t = c_ref.shape[1]

    @pl.when(ki == 0)
    def _():
        _init_state(m_ref, l_ref, acc_ref)

    def sweep(diagonal):
        c, r, ct = c_ref[0], r_ref[0], ct_ref[0]
        if diagonal:
            hidden = (lax.broadcasted_iota(jnp.int32, (t, t), 0) > lax.broadcasted_iota(jnp.int32, (t, t), 1))

        def body(g, carry):
            tiles = []
            for u in range(group):
                h = g * group + u
                s = _dot(c, qlt_ref[0, h]) + _dot(r, qrt_ref[0, h])
                tiles.append(jnp.where(hidden, NEG_INF, s) if diagonal else s)
            _softmax_step(jnp.stack(tiles), 1, _batched(lambda p: _dot(ct, p), group),
                          m_ref.at[g], l_ref.at[g], acc_ref.at[g], scale)
            return carry

        lax.fori_loop(0, n_groups, body, 0)

    pl.when(ki < qi)(lambda: sweep(False))

    @pl.when(ki == qi)
    def _():
        sweep(True)

        def body(g, carry):
            o = (acc_ref[g] / l_ref[g]).astype(o_ref.dtype)
            for u in range(group):
                o_ref[0, g * group + u] = o[u]
            return carry

        lax.fori_loop(0, n_groups, body, 0)


def _mla_prefill(qlt, qrt, c, r, ct, scale, t):
    b, h, dc, s = qlt.shape
    dr = qrt.shape[2]
    nq = s // t
    g = _pick(h, (HEAD_GROUP,))
    q_map = lambda bi, qi, ki: (bi, 0, 0, qi)
    k_map = lambda bi, qi, ki: (bi, jnp.minimum(ki, qi), 0)
    return pl.pallas_call(
        functools.partial(_mla_prefill_kernel, scale=scale),
        grid=(b, nq, nq),
        in_specs=[pl.BlockSpec((1, h, dc, t), q_map), pl.BlockSpec((1, h, dr, t), q_map),
                  pl.BlockSpec((1, t, dc), k_map), pl.BlockSpec((1, t, dr), k_map),
                  pl.BlockSpec((1, dc, t), lambda bi, qi, ki: (bi, 0, jnp.minimum(ki, qi)))],
        out_specs=pl.BlockSpec((1, h, dc, t), q_map),
        out_shape=jax.ShapeDtypeStruct((b, h, dc, s), BF16),
        scratch_shapes=[pltpu.VMEM((h // g, g, 1, t), F32), pltpu.VMEM((h // g, g, 1, t), F32),
                        pltpu.VMEM((h // g, g, dc, t), F32)],
        compiler_params=_cparams(3),
    )(qlt, qrt, c, r, ct)


def _page_copies(pt_ref, caches, bufs, sems, layer, seq, chunk, slot):
    n_pages = bufs[0].shape[1]
    copies = []
    for j in range(n_pages):
        pg = pt_ref[seq, chunk * n_pages + j]
        for a, (cache, buf) in enumerate(zip(caches, bufs)):
            copies.append(pltpu.make_async_copy(cache.at[layer, pg], buf.at[slot, j], sems.at[slot, a]))
    return copies


def _decode_kernel(pt_ref, lam_ref, q2_ref, ql_ref, qr_ref, kt_hbm, v_hbm, mt_hbm,
                   ktn_ref, vn_ref, mtn_ref, bias_last_ref, bias_new_ref, mask_new_ref, od_ref, om_ref,
                   kt_buf, v_buf, mt_buf, sems, md_ref, ld_ref, ad_ref, mm_ref, lm_ref, am_ref, *,
                   layer, scale, dc):
    b, c = pl.program_id(0), pl.program_id(1)
    nb, nc = pl.num_programs(0), pl.num_programs(1)
    n_pages = kt_buf.shape[1]
    q2, ql, qr = q2_ref[0], ql_ref[0], qr_ref[0]
    diff_state = (md_ref, ld_ref, ad_ref)
    mla_state = (mm_ref, lm_ref, am_ref)
    copies = functools.partial(_page_copies, pt_ref, (kt_hbm, v_hbm, mt_hbm), (kt_buf, v_buf, mt_buf), sems, layer)

    step = b * nc + c
    slot = step % 2

    @pl.when(step == 0)
    def _():
        for cp in copies(b, c, slot):
            cp.start()

    @pl.when(step + 1 < nb * nc)
    def _():
        wrap = c + 1 == nc
        for cp in copies(jnp.where(wrap, b + 1, b), jnp.where(wrap, 0, c + 1), 1 - slot):
            cp.start()

    @pl.when(c == 0)
    def _():
        _init_state(*diff_state, *mla_state)

    for cp in copies(b, c, slot):
        cp.wait()

    kt = jnp.concatenate([kt_buf[slot, j].astype(BF16) for j in range(n_pages)], axis=1)
    v = jnp.concatenate([v_buf[slot, j].astype(BF16) for j in range(n_pages)], axis=0)
    mt = jnp.concatenate([mt_buf[slot, j].astype(BF16) for j in range(n_pages)], axis=1)
    page = bias_last_ref.shape[-1]

    s_d = _dot(q2, kt)
    s_d = jnp.concatenate([s_d[:, :-page], s_d[:, -page:] + bias_last_ref[0]], axis=1)
    _softmax_step(s_d, 1, lambda p: _dot(p, v), *diff_state)
    ct = mt[:dc]
    s_m = _dot(ql, ct) + _dot(qr, mt[dc:])
    _softmax_step(s_m, 1, lambda p: _dot_nt(p, ct), *mla_state, scale)

    @pl.when(c == nc - 1)
    def _():
        vn = vn_ref[0]
        _softmax_step(_dot(q2, ktn_ref[0]) + bias_new_ref[...], 1, lambda p: _dot(p, vn), *diff_state)
        o = ad_ref[...] / ld_ref[...]
        half = o.shape[0] // 2
        od_ref[0] = o[:half] - lam_ref[0, 0] * o[half:]
        mtn = mtn_ref[0]
        ctn = mtn[:dc]
        s_n = _dot(ql, ctn) + _dot(qr, mtn[dc:]) + mask_new_ref[...]
        _softmax_step(s_n, 1, lambda p: _dot_nt(p, ctn), *mla_state, scale)
        om_ref[0] = (am_ref[...] / lm_ref[...]).astype(om_ref.dtype)


def _decode_attention(page_table, lam, q2, ql, qr, cache_kt, cache_v, cache_mt, layer,
                      kt_new, v_new, mt_new, bias_last, bias_new, mask_new, scale):
    b, r2, d2 = q2.shape
    r, dc = ql.shape[1:]
    dr = qr.shape[-1]
    n_tab = page_table.shape[1]
    page = cache_v.shape[2]
    p = PAGES_PER_STEP
    assert n_tab % p == 0
    nc = n_tab // p
    const = lambda bi, ci, pt: (0, 0)
    seq = lambda bi, ci, pt: (bi, 0, 0)
    hbm = pl.BlockSpec(memory_space=pl.ANY)

    grid_spec = pltpu.PrefetchScalarGridSpec(
        num_scalar_prefetch=1,
        grid=(b, nc),
        in_specs=[pl.BlockSpec(memory_space=pltpu.SMEM),
                  pl.BlockSpec((1, r2, d2), seq), pl.BlockSpec((1, r, dc), seq), pl.BlockSpec((1, r, dr), seq),
                  hbm, hbm, hbm,
                  pl.BlockSpec((1, d2, page), seq), pl.BlockSpec((1, page, d2), seq),
                  pl.BlockSpec((1, dc + dr, page), seq),
                  pl.BlockSpec((1, r2, page), lambda bi, ci, pt: (jnp.where(ci == nc - 1, 1, 0), 0, 0)),
                  pl.BlockSpec((r2, page), const), pl.BlockSpec((r, page), const)],
        out_specs=[pl.BlockSpec((1, r2 // 2, d2), seq), pl.BlockSpec((1, r, dc), seq)],
        scratch_shapes=[pltpu.VMEM((2, p, d2, page), F32), pltpu.VMEM((2, p, page, d2), F32),
                        pltpu.VMEM((2, p, dc + dr, page), F32), pltpu.SemaphoreType.DMA((2, 3)),
                        pltpu.VMEM((r2, 1), F32), pltpu.VMEM((r2, 1), F32), pltpu.VMEM((r2, d2), F32),
                        pltpu.VMEM((r, 1), F32), pltpu.VMEM((r, 1), F32), pltpu.VMEM((r, dc), F32)],
    )
    return pl.pallas_call(
        functools.partial(_decode_kernel, layer=layer, scale=scale, dc=dc),
        grid_spec=grid_spec,
        out_shape=[jax.ShapeDtypeStruct((b, r2 // 2, d2), F32), jax.ShapeDtypeStruct((b, r, dc), BF16)],
        compiler_params=_cparams(2),
    )(page_table, lam, q2, ql, qr, cache_kt, cache_v, cache_mt,
      kt_new, v_new, mt_new, bias_last, bias_new, mask_new)


def _rmsnorm(x, g):
    return x * lax.rsqrt(jnp.mean(x * x, axis=-1, keepdims=True) + NORM_EPS) * g


def _rope(x, pos):
    half = x.shape[-1] // 2
    inv = ROPE_THETA ** (-jnp.arange(half, dtype=F32) / half)
    ang = pos.astype(F32)[:, None] * inv
    ang = ang.reshape(ang.shape[:1] + (1,) * (x.ndim - 3) + (half,))
    cos, sin = jnp.cos(ang), jnp.sin(ang)
    x1, x2 = x[..., :half], x[..., half:]
    return jnp.concatenate([x1 * cos - x2 * sin, x1 * sin + x2 * cos], axis=-1)


def _half_masked(q, axis):
    d2 = q.shape[axis]
    idx = lax.broadcasted_iota(jnp.int32, q.shape, axis)
    zero = jnp.zeros_like(q)
    return jnp.stack([jnp.where(idx < d2 // 2, q, zero), jnp.where(idx >= d2 // 2, q, zero)], axis=axis)


def _layer(x, xb, pos, past, layer, w, bias, dims, shape):
    b, s = shape
    n, d = x.shape
    h_diff, d2, h_mla, d_nope, d_rope, dc, d_v, q_lora, d_ff = dims
    alpha, mla_scale = w["alpha"], w["mla_scale"]
    tm = _pick(n, (512, 256, 128))

    proj = _matmul(xb, w["w_in"], layer, F32, tm, _pick(w["w_in"].shape[2], (1152, 384, 128)))
    o = 0
    dq = proj[:, o:o + h_diff * d2]; o += h_diff * d2
    dk = proj[:, o:o + d2]; o += d2
    dv = proj[:, o:o + d2]; o += d2
    cq = proj[:, o:o + q_lora]; o += q_lora
    ckv = proj[:, o:o + dc]; o += dc
    kr = proj[:, o:o + d_rope]

    c_kv = _rmsnorm(ckv, w["kv_norm"])
    k_r = _rope(kr.reshape(b, s, d_rope), pos).reshape(n, d_rope)
    new_mla = jnp.concatenate([c_kv, k_r], axis=-1)

    q = _matmul(_rmsnorm(cq, w["q_norm"]).astype(BF16), w["w_q_up"], layer, F32, tm, 1024)
    q_nope = q[:, :h_mla * d_nope].astype(BF16)
    q_rope = _rope(q[:, h_mla * d_nope:].reshape(b, s, h_mla, d_rope), pos).astype(BF16)
    dq_b = (dq * (d2 // 2) ** -0.5).astype(BF16)
    dk_b = dk.astype(BF16).reshape(b, s, d2)
    dv_b = dv.astype(BF16).reshape(b, s, d2)
    new_mla_b = new_mla.astype(BF16).reshape(b, s, dc + d_rope)
    lam = w["lam"]

    if past is None:
        q_lat_t = _head_matmul(
            w["w_uk"], q_nope.reshape(b, s, h_mla * d_nope), (b, h_mla, s // tm),
            pl.BlockSpec((None, dc, d_nope), lambda bi, hi, i: (hi, 0, 0)),
            pl.BlockSpec((None, tm, d_nope), lambda bi, hi, i: (bi, i, hi)),
            pl.BlockSpec((None, None, dc, tm), lambda bi, hi, i: (bi, hi, 0, i)),
            jax.ShapeDtypeStruct((b, h_mla, dc, s), BF16), body=_mm_nt_kernel)
        qt = _half_masked(dq_b.reshape(b, s, h_diff, d2).transpose(0, 2, 3, 1), 2)
        diff_o = _diff_prefill(qt, dk_b, dv_b.transpose(0, 2, 1), bias["prefill"], lam)
        diff_o = diff_o.transpose(0, 3, 1, 2)
        mla_lat = _mla_prefill(q_lat_t, q_rope.transpose(0, 2, 3, 1),
                               new_mla_b[..., :dc], new_mla_b[..., dc:],
                               new_mla_b[..., :dc].transpose(0, 2, 1), mla_scale, ATTN_TILE)
        mla_o = _head_matmul(
            mla_lat.transpose(0, 1, 3, 2), w["w_uv"], (b, h_mla, s // tm),
            pl.BlockSpec((None, None, tm, dc), lambda bi, hi, i: (bi, hi, i, 0)),
            pl.BlockSpec((None, dc, d_v), lambda bi, hi, i: (hi, 0, 0)),
            pl.BlockSpec((None, tm, d_v), lambda bi, hi, i: (bi, i, hi)),
            jax.ShapeDtypeStruct((b, s, h_mla * d_v), F32)).reshape(n, h_mla * d_v)
    else:
        page_table, cache_kt, cache_v, cache_mt, conv_prev = past
        page = cache_v.shape[2]
        q_lat = _head_matmul(
            q_nope, w["w_uk"], (h_mla, n // tm),
            pl.BlockSpec((tm, d_nope), lambda hi, i: (i, hi)),
            pl.BlockSpec((None, dc, d_nope), lambda hi, i: (hi, 0, 0)),
            pl.BlockSpec((tm, dc), lambda hi, i: (i, hi)),
            jax.ShapeDtypeStruct((n, h_mla * dc), BF16), body=_mm_nt_kernel)
        q2 = _half_masked(dq_b.reshape(b, s * h_diff, d2), 2).transpose(0, 2, 1, 3)
        q2 = q2.reshape(b, 2 * s * h_diff, d2)

        def pad_new(a):
            return jnp.pad(a, ((0, 0), (0, page - s), (0, 0)))

        diff_o, mla_lat = _decode_attention(
            page_table, lam, q2, q_lat.reshape(b, s * h_mla, dc), q_rope.reshape(b, s * h_mla, d_rope),
            cache_kt, cache_v, cache_mt, layer,
            pad_new(dk_b).transpose(0, 2, 1), pad_new(dv_b), pad_new(new_mla_b).transpose(0, 2, 1),
            bias["dec_last"], bias["dec_new"], bias["dec_mask"], mla_scale)
        diff_o = diff_o.reshape(b, s, h_diff, d2)
        mla_o = _head_matmul(
            mla_lat.reshape(n, h_mla * dc), w["w_uv"], (h_mla, n // tm),
            pl.BlockSpec((tm, dc), lambda hi, i: (i, hi)),
            pl.BlockSpec((None, dc, d_v), lambda hi, i: (hi, 0, 0)),
            pl.BlockSpec((tm, d_v), lambda hi, i: (i, hi)),
            jax.ShapeDtypeStruct((n, h_mla * d_v), F32))

    diff_o = _rmsnorm(diff_o, w["subln"]) * (1.0 - w["lam_init"])
    mix = jnp.concatenate([diff_o.reshape(n, h_diff * d2), mla_o], axis=-1).astype(BF16)
    attn = _matmul(mix, w["w_o"], layer, F32, tm, 1024)
    x1, x1b = _add_layernorm(x, attn, w["ln1_g"], w["ln1_b"], alpha)

    f_pad = w["w_gate"].shape[2]
    if past is None:
        prev = jnp.zeros((b, SUBLANE, f_pad), F32)
        hid, tail = _ffn_in(x1b, w["w_gate"], w["w_up"], layer, w["conv_w"], w["conv_b"], prev, s, tm)
        new_conv = tail[:, SUBLANE - (CONV_W - 1):, :d_ff]
    else:
        g_pre = _matmul(x1b, w["w_gate"], layer, F32, tm, FF_TILE).reshape(b, s, f_pad)
        u = _matmul(x1b, w["w_up"], layer, F32, tm, FF_TILE).reshape(b, s, f_pad)
        padded = jnp.concatenate([jnp.pad(conv_prev, ((0, 0), (0, 0), (0, f_pad - d_ff))), g_pre], axis=1)
        g = w["conv_b"] + sum(w["conv_w"][j] * padded[:, j:j + s] for j in range(CONV_W))
        hid = (jax.nn.gelu(g, approximate=False) * u).astype(BF16).reshape(n, f_pad)
        new_conv = padded[:, s:, :d_ff]
    down = _matmul(hid, w["w_down"], layer, F32, 256, 512)
    x2, x2b = _add_layernorm(x1, down, w["ln2_g"], w["ln2_b"], alpha)
    return x2, x2b, (dk, dv, new_mla, new_conv)


def kernel(x_prompt, x_sample, cache_diff_k, cache_diff_v, cache_mla, state_conv, page_table, w_in, q_norm, w_q_up, kv_norm, w_kv_up, lam_q1, lam_k1, lam_q2, lam_k2, subln, w_o, ln1_g, ln1_b, w_gate, w_up, conv_w, conv_b, w_down, ln2_g, ln2_b, rel_table):
    depth, d_model, in_cols = w_in.shape
    b_p, s_p, _ = x_prompt.shape
    b_s, s_s, _ = x_sample.shape
    n_pool, page = cache_diff_k.shape[1:3]
    d2 = cache_diff_v.shape[-1]
    q_lora, h_mla, qk_dim = w_q_up.shape[1:]
    dc = w_kv_up.shape[1]
    d_rope = cache_mla.shape[-1] - dc
    d_nope = qk_dim - d_rope
    d_v = w_kv_up.shape[-1] - d_nope
    h_diff = rel_table.shape[1]
    d_ff = w_gate.shape[-1]
    dims = (h_diff, d2, h_mla, d_nope, d_rope, dc, d_v, q_lora, d_ff)
    past_len = page_table.shape[1] * page
    t = ATTN_TILE
    assert page >= REL_MAX_DIST and t >= REL_MAX_DIST and page % LANE == 0 and s_s <= SUBLANE
    assert cache_diff_k.shape[3] == 1 and cache_diff_v.shape[3] == 1, "one shared key/value head"

    pos_p = jnp.arange(s_p, dtype=jnp.int32)
    pos_s = past_len + jnp.arange(s_s, dtype=jnp.int32)

    kk = jnp.arange(t, dtype=jnp.int32)[:, None]
    qq = jnp.arange(t, dtype=jnp.int32)[None, :]
    bias_p = _bias_from_dist(rel_table, jnp.stack([qq - kk, t + qq - kk]))
    tt = jnp.arange(SUBLANE, dtype=jnp.int32)[:, None]
    jj = jnp.arange(page, dtype=jnp.int32)[None, :]
    new_ok = jnp.logical_and(jj <= tt, jj < s_s)
    bias_s = _bias_from_dist(rel_table, jnp.stack([page + tt - jj, jnp.where(new_ok, tt - jj, -1)]))
    bias_s = bias_s[:, :, :s_s].transpose(0, 2, 1, 3).reshape(2, s_s * h_diff, page)
    bias_s = jnp.concatenate([bias_s, bias_s], axis=1)
    bias = {
        "prefill": bias_p,
        "dec_last": jnp.stack([jnp.zeros_like(bias_s[0]), bias_s[0]]),
        "dec_new": bias_s[1],
        "dec_mask": jnp.where(jnp.repeat(new_ok[:s_s], h_mla, axis=0), 0.0, NEG_INF).astype(F32),
    }

    cache_kt = cache_diff_k.transpose(0, 1, 3, 4, 5, 2).reshape(depth, n_pool, d2, page)
    cache_v = cache_diff_v.reshape(depth, n_pool, page, d2)
    cache_mt = cache_mla.transpose(0, 1, 3, 2)

    f_pad = -(-d_ff // FF_TILE) * FF_TILE
    pad_f = f_pad - d_ff
    big = {
        "w_in": jnp.pad(w_in, ((0, 0), (0, 0), (0, (-in_cols) % LANE))).astype(BF16),
        "w_q_up": jnp.concatenate([w_q_up[..., :d_nope].reshape(depth, q_lora, -1),
                                   w_q_up[..., d_nope:].reshape(depth, q_lora, -1)], axis=2).astype(BF16),
        "w_o": w_o.astype(BF16),
        "w_gate": jnp.pad(w_gate, ((0, 0), (0, 0), (0, pad_f))).astype(BF16),
        "w_up": jnp.pad(w_up, ((0, 0), (0, 0), (0, pad_f))).astype(BF16),
        "w_down": jnp.pad(w_down, ((0, 0), (0, pad_f), (0, 0))).astype(BF16),
    }
    y_p, y_s = x_prompt.reshape(b_p * s_p, d_model), x_sample.reshape(b_s * s_s, d_model)
    yb_p, yb_s = y_p.astype(BF16), y_s.astype(BF16)
    st_p, st_s = [], []
    for l in range(depth):
        lam_init = 0.8 - 0.6 * math.exp(-0.3 * l)
        lam = (jnp.exp(jnp.sum(lam_q1[l] * lam_k1[l])) - jnp.exp(jnp.sum(lam_q2[l] * lam_k2[l])) + lam_init)
        w = dict(big)
        w.update({
            "alpha": (2 * depth) ** 0.25,
            "mla_scale": float(qk_dim) ** -0.5,
            "lam": lam.reshape(1, 1).astype(F32),
            "lam_init": lam_init,
            "w_uk": w_kv_up[l][:, :, :d_nope].transpose(1, 0, 2).astype(BF16),
            "w_uv": w_kv_up[l][:, :, d_nope:].transpose(1, 0, 2).astype(BF16),
            "conv_w": jnp.pad(conv_w[l], ((0, 0), (0, pad_f))),
            "conv_b": jnp.pad(conv_b[l], (0, pad_f)).reshape(1, f_pad),
            "q_norm": q_norm[l], "kv_norm": kv_norm[l], "subln": subln[l],
            "ln1_g": ln1_g[l], "ln1_b": ln1_b[l], "ln2_g": ln2_g[l], "ln2_b": ln2_b[l],
        })
        y_p, yb_p, sp = _layer(y_p, yb_p, pos_p, None, l, w, bias, dims, (b_p, s_p))
        y_s, yb_s, ss = _layer(y_s, yb_s, pos_s, (page_table, cache_kt, cache_v, cache_mt, state_conv[l]),
                               l, w, bias, dims, (b_s, s_s))
        st_p.append(sp)
        st_s.append(ss)

    def stack(states, i, shape):
        return jnp.stack([st[i].reshape(shape) for st in states])

    outs = [y_p.reshape(x_prompt.shape), y_s.reshape(x_sample.shape)]
    for states, (bb, ss_) in ((st_p, (b_p, s_p)), (st_s, (b_s, s_s))):
        outs += [stack(states, 0, (bb, ss_, 1, 2, d2 // 2)), stack(states, 1, (bb, ss_, 1, d2)),
                 stack(states, 2, (bb, ss_, dc + d_rope)), stack(states, 3, (bb, CONV_W - 1, d_ff))]
    return tuple(outs)
```

```python
import functools
import math

import numpy as np
import jax
import jax.numpy as jnp
from jax import lax
from jax.experimental import pallas as pl
from jax.experimental.pallas import tpu as pltpu

F32 = jnp.float32
BF16 = jnp.bfloat16

ROPE_THETA = 10000.0
REL_MAX_DIST = 128
NORM_EPS = 1e-5
NEG_INF = -1e30
M_INIT = -1e37
CONV_W = 3

V7X_VMEM_LIMIT = 56 * 1024 * 1024
LANE = 128
SUBLANE = 8
ATTN_TILE = 256
HEAD_GROUP = 4
PAGES_PER_STEP = 32
FF_TILE = 512


def _cparams(n_axes):
    return pltpu.CompilerParams(dimension_semantics=("arbitrary",) * n_axes,
                                vmem_limit_bytes=V7X_VMEM_LIMIT)


def _pick(n, cands):
    for c in cands:
        if n % c == 0:
            return c
    return n


def _mm_kernel(x_ref, w_ref, o_ref):
    o_ref[...] = jnp.dot(x_ref[...], w_ref[...], preferred_element_type=F32).astype(o_ref.dtype)


def _mm_nt_kernel(x_ref, w_ref, o_ref):
    o_ref[...] = lax.dot_general(x_ref[...], w_ref[...], (((1,), (1,)), ((), ())),
                                 preferred_element_type=F32).astype(o_ref.dtype)


def _matmul(x, w, layer, out_dtype, tm, tn):
    m, k = x.shape
    n = w.shape[2]
    tm, tn = _pick(m, (tm, 256, 128)), _pick(n, (tn, 512, 256, 128))
    return pl.pallas_call(
        _mm_kernel,
        grid=(n // tn, m // tm),
        in_specs=[pl.BlockSpec((tm, k), lambda j, i: (i, 0)),
                  pl.BlockSpec((None, k, tn), lambda j, i: (layer, 0, j))],
        out_specs=pl.BlockSpec((tm, tn), lambda j, i: (i, j)),
        out_shape=jax.ShapeDtypeStruct((m, n), out_dtype),
        compiler_params=_cparams(2),
    )(x, w)


def _head_matmul(x, w, grid, x_spec, w_spec, o_spec, out_shape, body=_mm_kernel):
    return pl.pallas_call(
        body, grid=grid, in_specs=[x_spec, w_spec], out_specs=o_spec,
        out_shape=out_shape, compiler_params=_cparams(len(grid)),
    )(x, w)


def _latent_query_kernel(w_ref, x_ref, o_ref):
    n_heads, _, d_nope = w_ref.shape
    for h in range(n_heads):
        o_ref[0, h] = _dot_nt(w_ref[h], x_ref[0, :, h * d_nope:(h + 1) * d_nope]).astype(o_ref.dtype)


def _latent_queries_t(q_nope, w_uk, tm):
    b, s, _ = q_nope.shape
    h, dc, d_nope = w_uk.shape
    return pl.pallas_call(
        _latent_query_kernel,
        grid=(b, s // tm),
        in_specs=[pl.BlockSpec((h, dc, d_nope), lambda bi, i: (0, 0, 0)),
                  pl.BlockSpec((1, tm, h * d_nope), lambda bi, i: (bi, i, 0))],
        out_specs=pl.BlockSpec((1, h, dc, tm), lambda bi, i: (bi, 0, 0, i)),
        out_shape=jax.ShapeDtypeStruct((b, h, dc, s), BF16),
        compiler_params=_cparams(2),
    )(w_uk, q_nope)


def _latent_value_kernel(x_ref, w_ref, o_ref):
    n_heads, _, d_v = w_ref.shape
    for h in range(n_heads):
        o_ref[0, :, h * d_v:(h + 1) * d_v] = jnp.dot(x_ref[0, h], w_ref[h], preferred_element_type=F32)


def _latent_values(lat, w_uv, tm):
    b, h, s, dc = lat.shape
    d_v = w_uv.shape[2]
    return pl.pallas_call(
        _latent_value_kernel,
        grid=(b, s // tm),
        in_specs=[pl.BlockSpec((1, h, tm, dc), lambda bi, i: (bi, 0, i, 0)),
                  pl.BlockSpec((h, dc, d_v), lambda bi, i: (0, 0, 0))],
        out_specs=pl.BlockSpec((1, tm, h * d_v), lambda bi, i: (bi, i, 0)),
        out_shape=jax.ShapeDtypeStruct((b, s, h * d_v), F32),
        compiler_params=_cparams(2),
    )(lat, w_uv)


def _add_ln_kernel(x_ref, y_ref, g_ref, b_ref, o_ref, ob_ref, *, alpha):
    z = alpha * x_ref[...] + y_ref[...]
    zc = z - jnp.mean(z, axis=-1, keepdims=True)
    var = jnp.mean(zc * zc, axis=-1, keepdims=True)
    o = zc * lax.rsqrt(var + NORM_EPS) * g_ref[...] + b_ref[...]
    o_ref[...] = o
    ob_ref[...] = o.astype(ob_ref.dtype)


def _add_layernorm(x, y, g, b, alpha):
    m, d = x.shape
    tr = _pick(m, (256, 128))
    row = pl.BlockSpec((tr, d), lambda i: (i, 0))
    vec = pl.BlockSpec((1, d), lambda i: (0, 0))
    return pl.pallas_call(
        functools.partial(_add_ln_kernel, alpha=alpha),
        grid=(m // tr,),
        in_specs=[row, row, vec, vec],
        out_specs=[row, row],
        out_shape=[jax.ShapeDtypeStruct((m, d), F32), jax.ShapeDtypeStruct((m, d), BF16)],
        compiler_params=_cparams(1),
    )(x, y, g.reshape(1, d), b.reshape(1, d))


def _ffn_in_kernel(x_ref, wg_ref, wu_ref, cw_ref, cb_ref, prev_ref, h_ref, tail_ref, carry_ref, *,
                   tiles_per_seq):
    i = pl.program_id(1)
    x = x_ref[...]
    g = jnp.dot(x, wg_ref[...], preferred_element_type=F32)
    u = jnp.dot(x, wu_ref[...], preferred_element_type=F32)
    prev = jnp.where(i % tiles_per_seq == 0, prev_ref[0], carry_ref[...])
    row = lax.broadcasted_iota(jnp.int32, g.shape, 0)
    g1 = jnp.where(row == 0, prev[7:8], pltpu.roll(g, 1, 0))
    g2 = jnp.where(row == 0, prev[6:7], jnp.where(row == 1, prev[7:8], pltpu.roll(g, 2, 0)))
    cw = cw_ref[...]
    gc = cb_ref[...] + cw[0:1] * g2 + cw[1:2] * g1 + cw[2:3] * g
    act = 0.5 * gc * (1.0 + lax.erf(gc * np.float32(math.sqrt(0.5))))
    h_ref[...] = (act * u).astype(h_ref.dtype)
    tail = g[g.shape[0] - SUBLANE:]
    carry_ref[...] = tail
    tail_ref[0] = tail


def _ffn_in(x, w_gate, w_up, layer, conv_w, conv_b, prev, seq_len, tm):
    n, d = x.shape
    f = w_gate.shape[2]
    tn = FF_TILE
    assert seq_len % tm == 0 and f % tn == 0 and tm >= SUBLANE and CONV_W == 3
    tps = seq_len // tm
    col = lambda j, i: (0, j)
    wcol = pl.BlockSpec((None, d, tn), lambda j, i: (layer, 0, j))
    return pl.pallas_call(
        functools.partial(_ffn_in_kernel, tiles_per_seq=tps),
        grid=(f // tn, n // tm),
        in_specs=[pl.BlockSpec((tm, d), lambda j, i: (i, 0)), wcol, wcol,
                  pl.BlockSpec((CONV_W, tn), col), pl.BlockSpec((1, tn), col),
                  pl.BlockSpec((1, SUBLANE, tn), lambda j, i: (i // tps, 0, j))],
        out_specs=[pl.BlockSpec((tm, tn), lambda j, i: (i, j)),
                   pl.BlockSpec((1, SUBLANE, tn), lambda j, i: (i // tps, 0, j))],
        out_shape=[jax.ShapeDtypeStruct((n, f), BF16),
                   jax.ShapeDtypeStruct((n // seq_len, SUBLANE, f), F32)],
        scratch_shapes=[pltpu.VMEM((SUBLANE, tn), F32)],
        compiler_params=_cparams(2),
    )(x, w_gate, w_up, conv_w, conv_b, prev)


def _bucket_uppers(n_buckets):
    max_exact = n_buckets // 2
    n = np.arange(0, REL_MAX_DIST + 1)
    nf = np.maximum(n, 1).astype(np.float64)
    v = np.log(nf / max_exact) / math.log(REL_MAX_DIST / max_exact) * (n_buckets - max_exact)
    frac = np.abs(v - np.round(v))[(n > max_exact) & (n < REL_MAX_DIST)]
    assert frac.min() > 1e-3, "bucket boundary too close to an integer distance"
    large = np.minimum(max_exact + np.floor(np.maximum(v + 1e-9, 0)).astype(np.int64), n_buckets - 1)
    bucket = np.where(n < max_exact, n, large)
    assert bucket[REL_MAX_DIST] == n_buckets - 1
    return [int(n[bucket <= b].max()) for b in range(n_buckets - 1)]


def _bias_kernel(tab_ref, n_ref, o_ref, *, uppers):
    h = pl.program_id(1)
    n = n_ref[0]
    far = tab_ref[len(uppers), h]
    val = jnp.zeros(n.shape, F32)
    for b in range(len(uppers) - 1, -1, -1):
        val = jnp.where(n <= uppers[b], tab_ref[b, h] - far, val)
    o_ref[0, 0] = jnp.where(n < 0, NEG_INF, val)


def _bias_from_dist(rel_table, dist):
    g, r, c = dist.shape
    n_buckets, n_heads = rel_table.shape
    return pl.pallas_call(
        functools.partial(_bias_kernel, uppers=_bucket_uppers(n_buckets)),
        grid=(g, n_heads),
        in_specs=[pl.BlockSpec(memory_space=pltpu.SMEM),
                  pl.BlockSpec((1, r, c), lambda i, h: (i, 0, 0))],
        out_specs=pl.BlockSpec((1, 1, r, c), lambda i, h: (i, h, 0, 0)),
        out_shape=jax.ShapeDtypeStruct((g, n_heads, r, c), F32),
        compiler_params=_cparams(2),
    )(rel_table.astype(F32), dist)


_NT = (((1,), (1,)), ((), ()))


def _dot(a, b):
    return jnp.dot(a, b, preferred_element_type=F32)


def _dot_nt(a, b):
    return lax.dot_general(a, b, _NT, preferred_element_type=F32)


def _softmax_step(s, axis, pv, m_r, l_r, a_r, scale=None):
    m_old = m_r[...]
    m_new = jnp.maximum(m_old, jnp.max(s, axis=axis, keepdims=True))
    dm, ds = m_old - m_new, s - m_new
    if scale is not None:
        dm, ds = dm * scale, ds * scale
    alpha = jnp.exp(dm)
    p = jnp.exp(ds)
    l_r[...] = alpha * l_r[...] + jnp.sum(p, axis=axis, keepdims=True)
    a_r[...] = alpha * a_r[...] + pv(p.astype(BF16))
    m_r[...] = m_new


def _init_state(*state_refs):
    for m_ref, l_ref, acc_ref in zip(state_refs[0::3], state_refs[1::3], state_refs[2::3]):
        m_ref[...] = jnp.full(m_ref.shape, M_INIT, F32)
        l_ref[...] = jnp.zeros(l_ref.shape, F32)
        acc_ref[...] = jnp.zeros(acc_ref.shape, F32)


def _batched(fn, n):
    return lambda p: jnp.stack([fn(p[i]) for i in range(n)])


def _diff_prefill_kernel(lam_ref, q_ref, k_ref, vt_ref, bias_ref, o_ref, m_ref, l_ref, acc_ref):
    qi, ki = pl.program_id(1), pl.program_id(2)
    n_groups, n_tiles = m_ref.shape[:2]
    group = n_tiles // 2

    @pl.when(ki == 0)
    def _():
        _init_state(m_ref, l_ref, acc_ref)

    def sweep(near):
        k = k_ref[0]
        vt = vt_ref[0]

        def body(g, carry):
            tiles = []
            for u in range(group):
                h = g * group + u
                for j in range(2):
                    s = _dot(k, q_ref[0, h, j])
                    tiles.append(s + bias_ref[0, h] if near else s)
            _softmax_step(jnp.stack(tiles), 1, _batched(lambda p: _dot(vt, p), n_tiles),
                          m_ref.at[g], l_ref.at[g], acc_ref.at[g])
            return carry

        lax.fori_loop(0, n_groups, body, 0)

    pl.when(ki < qi - 1)(lambda: sweep(False))
    pl.when(jnp.logical_and(ki >= qi - 1, ki <= qi))(lambda: sweep(True))

    @pl.when(ki == qi)
    def _():
        lam = lam_ref[0, 0]

        def body(g, carry):
            o = acc_ref[g] / l_ref[g]
            for u in range(group):
                o_ref[0, g * group + u] = o[2 * u] - lam * o[2 * u + 1]
            return carry

        lax.fori_loop(0, n_groups, body, 0)


def _diff_prefill(qt, k, vt, bias, lam):
    b, h, _, d2, s = qt.shape
    t = bias.shape[-1]
    nq = s // t
    g = _pick(h, (HEAD_GROUP,))
    kblk = lambda bi, qi, ki: jnp.minimum(ki, qi)
    return pl.pallas_call(
        _diff_prefill_kernel,
        grid=(b, nq, nq),
        in_specs=[pl.BlockSpec(memory_space=pltpu.SMEM),
                  pl.BlockSpec((1, h, 2, d2, t), lambda bi, qi, ki: (bi, 0, 0, 0, qi)),
                  pl.BlockSpec((1, t, d2), lambda bi, qi, ki: (bi, kblk(bi, qi, ki), 0)),
                  pl.BlockSpec((1, d2, t), lambda bi, qi, ki: (bi, 0, kblk(bi, qi, ki))),
                  pl.BlockSpec((1, h, t, t), lambda bi, qi, ki: (jnp.clip(qi - ki, 0, 1), 0, 0, 0))],
        out_specs=pl.BlockSpec((1, h, d2, t), lambda bi, qi, ki: (bi, 0, 0, qi)),
        out_shape=jax.ShapeDtypeStruct((b, h, d2, s), F32),
        scratch_shapes=[pltpu.VMEM((h // g, 2 * g, 1, t), F32), pltpu.VMEM((h // g, 2 * g, 1, t), F32),
                        pltpu.VMEM((h // g, 2 * g, d2, t), F32)],
        compiler_params=_cparams(3),
    )(lam, qt, k, vt, bias)


def _mla_prefill_kernel(qlt_ref, qrt_ref, c_ref, r_ref, ct_ref, o_ref, m_ref, l_ref, acc_ref, *, scale):
    qi, ki = pl.program_id(1), pl.program_id(2)
    n_groups, group = m_ref.shape[:2]
    t = c_ref.shape[1]

    @pl.when(ki == 0)
    def _():
        _init_state(m_ref, l_ref, acc_ref)

    def sweep(diagonal):
        c, r, ct = c_ref[0], r_ref[0], ct_ref[0]
        if diagonal:
            hidden = (lax.broadcasted_iota(jnp.int32, (t, t), 0) > lax.broadcasted_iota(jnp.int32, (t, t), 1))

        def body(g, carry):
            tiles = []
            for u in range(group):
                h = g * group + u
                s = _dot(c, qlt_ref[0, h]) + _dot(r, qrt_ref[0, h])
                tiles.append(jnp.where(hidden, NEG_INF, s) if diagonal else s)
            _softmax_step(jnp.stack(tiles), 1, _batched(lambda p: _dot(ct, p), group),
                          m_ref.at[g], l_ref.at[g], acc_ref.at[g], scale)
            return carry

        lax.fori_loop(0, n_groups, body, 0)

    pl.when(ki < qi)(lambda: sweep(False))

    @pl.when(ki == qi)
    def _():
        sweep(True)

        def body(g, carry):
            o = (acc_ref[g] / l_ref[g]).astype(o_ref.dtype)
            for u in range(group):
                o_ref[0, g * group + u] = o[u]
            return carry

        lax.fori_loop(0, n_groups, body, 0)


def _mla_prefill(qlt, qrt, c, r, ct, scale, t):
    b, h, dc, s = qlt.shape
    dr = qrt.shape[2]
    nq = s // t
    g = _pick(h, (HEAD_GROUP,))
    q_map = lambda bi, qi, ki: (bi, 0, 0, qi)
    k_map = lambda bi, qi, ki: (bi, jnp.minimum(ki, qi), 0)
    return pl.pallas_call(
        functools.partial(_mla_prefill_kernel, scale=scale),
        grid=(b, nq, nq),
        in_specs=[pl.BlockSpec((1, h, dc, t), q_map), pl.BlockSpec((1, h, dr, t), q_map),
                  pl.BlockSpec((1, t, dc), k_map), pl.BlockSpec((1, t, dr), k_map),
                  pl.BlockSpec((1, dc, t), lambda bi, qi, ki: (bi, 0, jnp.minimum(ki, qi)))],
        out_specs=pl.BlockSpec((1, h, dc, t), q_map),
        out_shape=jax.ShapeDtypeStruct((b, h, dc, s), BF16),
        scratch_shapes=[pltpu.VMEM((h // g, g, 1, t), F32), pltpu.VMEM((h // g, g, 1, t), F32),
                        pltpu.VMEM((h // g, g, dc, t), F32)],
        compiler_params=_cparams(3),
    )(qlt, qrt, c, r, ct)


def _page_copies(pt_ref, caches, bufs, sems, layer, seq, chunk, slot):
    n_pages = bufs[0].shape[1]
    copies = []
    for j in range(n_pages):
        pg = pt_ref[seq, chunk * n_pages + j]
        for a, (cache, buf) in enumerate(zip(caches, bufs)):
            copies.append(pltpu.make_async_copy(cache.at[layer, pg], buf.at[slot, j], sems.at[slot, a]))
    return copies


def _decode_kernel(pt_ref, lam_ref, q2_ref, ql_ref, qr_ref, kt_hbm, v_hbm, mt_hbm,
                   ktn_ref, vn_ref, mtn_ref, bias_last_ref, bias_new_ref, mask_new_ref, od_ref, om_ref,
                   kt_buf, v_buf, mt_buf, sems, md_ref, ld_ref, ad_ref, mm_ref, lm_ref, am_ref, *,
                   layer, scale, dc):
    b, c = pl.program_id(0), pl.program_id(1)
    nb, nc = pl.num_programs(0), pl.num_programs(1)
    n_pages = kt_buf.shape[1]
    q2, ql, qr = q2_ref[0], ql_ref[0], qr_ref[0]
    diff_state = (md_ref, ld_ref, ad_ref)
    mla_state = (mm_ref, lm_ref, am_ref)
    copies = functools.partial(_page_copies, pt_ref, (kt_hbm, v_hbm, mt_hbm), (kt_buf, v_buf, mt_buf), sems, layer)

    step = b * nc + c
    slot = step % 2
    last = step == nb * nc - 1

    @pl.when(step == 0)
    def _():
        for cp in copies(b, c, slot):
            cp.start()

    @pl.when(c == 0)
    def _():
        _init_state(*diff_state, *mla_state)

    for cp in copies(b, c, slot):
        cp.wait()

    kt = jnp.concatenate([kt_buf[slot, j].astype(BF16) for j in range(n_pages)], axis=1)
    v = jnp.concatenate([v_buf[slot, j].astype(BF16) for j in range(n_pages)], axis=0)
    mt = jnp.concatenate([mt_buf[slot, j].astype(BF16) for j in range(n_pages)], axis=1)
    page = bias_last_ref.shape[-1]

    wrap = c + 1 == nc
    next_b = jnp.where(last, b, jnp.where(wrap, b + 1, b))
    next_c = jnp.where(last, c, jnp.where(wrap, 0, c + 1))
    for cp in copies(next_b, next_c, 1 - slot):
        cp.start()

    s_d = _dot(q2, kt)
    s_d = jnp.concatenate([s_d[:, :-page], s_d[:, -page:] + bias_last_ref[0]], axis=1)
    _softmax_step(s_d, 1, lambda p: _dot(p, v), *diff_state)
    ct = mt[:dc]
    s_m = _dot(ql, ct) + _dot(qr, mt[dc:])
    _softmax_step(s_m, 1, lambda p: _dot_nt(p, ct), *mla_state, scale)

    @pl.when(c == nc - 1)
    def _():
        vn = vn_ref[0]
        _softmax_step(_dot(q2, ktn_ref[0]) + bias_new_ref[...], 1, lambda p: _dot(p, vn), *diff_state)
        o = ad_ref[...] / ld_ref[...]
        half = o.shape[0] // 2
        od_ref[0] = o[:half] - lam_ref[0, 0] * o[half:]
        mtn = mtn_ref[0]
        ctn = mtn[:dc]
        s_n = _dot(ql, ctn) + _dot(qr, mtn[dc:]) + mask_new_ref[...]
        _softmax_step(s_n, 1, lambda p: _dot_nt(p, ctn), *mla_state, scale)
        om_ref[0] = (am_ref[...] / lm_ref[...]).astype(om_ref.dtype)

    @pl.when(last)
    def _():
        for cp in copies(b, c, 1 - slot):
            cp.wait()


def _decode_attention(page_table, lam, q2, ql, qr, cache_kt, cache_v, cache_mt, layer,
                      kt_new, v_new, mt_new, bias_last, bias_new, mask_new, scale):
    b, r2, d2 = q2.shape
    r, dc = ql.shape[1:]
    dr = qr.shape[-1]
    n_tab = page_table.shape[1]
    page = cache_v.shape[2]
    p = PAGES_PER_STEP
    assert n_tab % p == 0
    nc = n_tab // p
    const = lambda bi, ci, pt: (0, 0)
    seq = lambda bi, ci, pt: (bi, 0, 0)
    hbm = pl.BlockSpec(memory_space=pl.ANY)

    grid_spec = pltpu.PrefetchScalarGridSpec(
        num_scalar_prefetch=1,
        grid=(b, nc),
        in_specs=[pl.BlockSpec(memory_space=pltpu.SMEM),
                  pl.BlockSpec((1, r2, d2), seq), pl.BlockSpec((1, r, dc), seq), pl.BlockSpec((1, r, dr), seq),
                  hbm, hbm, hbm,
                  pl.BlockSpec((1, d2, page), seq), pl.BlockSpec((1, page, d2), seq),
                  pl.BlockSpec((1, dc + dr, page), seq),
                  pl.BlockSpec((1, r2, page), lambda bi, ci, pt: (jnp.where(ci == nc - 1, 1, 0), 0, 0)),
                  pl.BlockSpec((r2, page), const), pl.BlockSpec((r, page), const)],
        out_specs=[pl.BlockSpec((1, r2 // 2, d2), seq), pl.BlockSpec((1, r, dc), seq)],
        scratch_shapes=[pltpu.VMEM((2, p, d2, page), F32), pltpu.VMEM((2, p, page, d2), F32),
                        pltpu.VMEM((2, p, dc + dr, page), F32), pltpu.SemaphoreType.DMA((2, 3)),
                        pltpu.VMEM((r2, 1), F32), pltpu.VMEM((r2, 1), F32), pltpu.VMEM((r2, d2), F32),
                        pltpu.VMEM((r, 1), F32), pltpu.VMEM((r, 1), F32), pltpu.VMEM((r, dc), F32)],
    )
    return pl.pallas_call(
        functools.partial(_decode_kernel, layer=layer, scale=scale, dc=dc),
        grid_spec=grid_spec,
        out_shape=[jax.ShapeDtypeStruct((b, r2 // 2, d2), F32), jax.ShapeDtypeStruct((b, r, dc), BF16)],
        compiler_params=_cparams(2),
    )(page_table, lam, q2, ql, qr, cache_kt, cache_v, cache_mt,
      kt_new, v_new, mt_new, bias_last, bias_new, mask_new)


def _rmsnorm(x, g):
    return x * lax.rsqrt(jnp.mean(x * x, axis=-1, keepdims=True) + NORM_EPS) * g


def _rope(x, pos):
    half = x.shape[-1] // 2
    inv = ROPE_THETA ** (-jnp.arange(half, dtype=F32) / half)
    ang = pos.astype(F32)[:, None] * inv
    ang = ang.reshape(ang.shape[:1] + (1,) * (x.ndim - 3) + (half,))
    cos, sin = jnp.cos(ang), jnp.sin(ang)
    x1, x2 = x[..., :half], x[..., half:]
    return jnp.concatenate([x1 * cos - x2 * sin, x1 * sin + x2 * cos], axis=-1)


def _half_masked(q, axis):
    d2 = q.shape[axis]
    idx = lax.broadcasted_iota(jnp.int32, q.shape, axis)
    zero = jnp.zeros_like(q)
    return jnp.stack([jnp.where(idx < d2 // 2, q, zero), jnp.where(idx >= d2 // 2, q, zero)], axis=axis)


def _layer(x, xb, pos, past, layer, w, bias, dims, shape):
    b, s = shape
    n, d = x.shape
    h_diff, d2, h_mla, d_nope, d_rope, dc, d_v, q_lora, d_ff = dims
    alpha, mla_scale = w["alpha"], w["mla_scale"]
    tm = _pick(n, (512, 256, 128))

    proj = _matmul(xb, w["w_in"], layer, F32, tm, _pick(w["w_in"].shape[2], (1152, 384, 128)))
    o = 0
    dq = proj[:, o:o + h_diff * d2]; o += h_diff * d2
    dk = proj[:, o:o + d2]; o += d2
    dv = proj[:, o:o + d2]; o += d2
    cq = proj[:, o:o + q_lora]; o += q_lora
    ckv = proj[:, o:o + dc]; o += dc
    kr = proj[:, o:o + d_rope]

    c_kv = _rmsnorm(ckv, w["kv_norm"])
    k_r = _rope(kr.reshape(b, s, d_rope), pos).reshape(n, d_rope)
    new_mla = jnp.concatenate([c_kv, k_r], axis=-1)

    q = _matmul(_rmsnorm(cq, w["q_norm"]).astype(BF16), w["w_q_up"], layer, F32, tm, 1024)
    q_nope = q[:, :h_mla * d_nope].astype(BF16)
    q_rope = _rope(q[:, h_mla * d_nope:].reshape(b, s, h_mla, d_rope), pos).astype(BF16)
    dq_b = (dq * (d2 // 2) ** -0.5).astype(BF16)
    dk_b = dk.astype(BF16).reshape(b, s, d2)
    dv_b = dv.astype(BF16).reshape(b, s, d2)
    new_mla_b = new_mla.astype(BF16).reshape(b, s, dc + d_rope)
    lam = w["lam"]

    if past is None:
        q_lat_t = _latent_queries_t(q_nope.reshape(b, s, h_mla * d_nope), w["w_uk"], tm)
        qt = _half_masked(dq_b.reshape(b, s, h_diff, d2).transpose(0, 2, 3, 1), 2)
        diff_o = _diff_prefill(qt, dk_b, dv_b.transpose(0, 2, 1), bias["prefill"], lam)
        diff_o = diff_o.transpose(0, 3, 1, 2)
        mla_lat = _mla_prefill(q_lat_t, q_rope.transpose(0, 2, 3, 1),
                               new_mla_b[..., :dc], new_mla_b[..., dc:],
                               new_mla_b[..., :dc].transpose(0, 2, 1), mla_scale, ATTN_TILE)
        mla_o = _latent_values(mla_lat.transpose(0, 1, 3, 2), w["w_uv"], tm).reshape(n, h_mla * d_v)
    else:
        page_table, cache_kt, cache_v, cache_mt, conv_prev = past
        page = cache_v.shape[2]
        q_lat = _head_matmul(
            q_nope, w["w_uk"], (h_mla, n // tm),
            pl.BlockSpec((tm, d_nope), lambda hi, i: (i, hi)),
            pl.BlockSpec((None, dc, d_nope), lambda hi, i: (hi, 0, 0)),
            pl.BlockSpec((tm, dc), lambda hi, i: (i, hi)),
            jax.ShapeDtypeStruct((n, h_mla * dc), BF16), body=_mm_nt_kernel)
        q2 = _half_masked(dq_b.reshape(b, s * h_diff, d2), 2).transpose(0, 2, 1, 3)
        q2 = q2.reshape(b, 2 * s * h_diff, d2)

        def pad_new(a):
            return jnp.pad(a, ((0, 0), (0, page - s), (0, 0)))

        diff_o, mla_lat = _decode_attention(
            page_table, lam, q2, q_lat.reshape(b, s * h_mla, dc), q_rope.reshape(b, s * h_mla, d_rope),
            cache_kt, cache_v, cache_mt, layer,
            pad_new(dk_b).transpose(0, 2, 1), pad_new(dv_b), pad_new(new_mla_b).transpose(0, 2, 1),
            bias["dec_last"], bias["dec_new"], bias["dec_mask"], mla_scale)
        diff_o = diff_o.reshape(b, s, h_diff, d2)
        mla_o = _head_matmul(
            mla_lat.reshape(n, h_mla * dc), w["w_uv"], (h_mla, n // tm),
            pl.BlockSpec((tm, dc), lambda hi, i: (i, hi)),
            pl.BlockSpec((None, dc, d_v), lambda hi, i: (hi, 0, 0)),
            pl.BlockSpec((tm, d_v), lambda hi, i: (i, hi)),
            jax.ShapeDtypeStruct((n, h_mla * d_v), F32))

    diff_o = _rmsnorm(diff_o, w["subln"]) * (1.0 - w["lam_init"])
    mix = jnp.concatenate([diff_o.reshape(n, h_diff * d2), mla_o], axis=-1).astype(BF16)
    attn = _matmul(mix, w["w_o"], layer, F32, tm, 1024)
    x1, x1b = _add_layernorm(x, attn, w["ln1_g"], w["ln1_b"], alpha)

    f_pad = w["w_gate"].shape[2]
    if past is None:
        prev = jnp.zeros((b, SUBLANE, f_pad), F32)
        hid, tail = _ffn_in(x1b, w["w_gate"], w["w_up"], layer, w["conv_w"], w["conv_b"], prev, s, tm)
        new_conv = tail[:, SUBLANE - (CONV_W - 1):, :d_ff]
    else:
        g_pre = _matmul(x1b, w["w_gate"], layer, F32, tm, FF_TILE).reshape(b, s, f_pad)
        u = _matmul(x1b, w["w_up"], layer, F32, tm, FF_TILE).reshape(b, s, f_pad)
        padded = jnp.concatenate([jnp.pad(conv_prev, ((0, 0), (0, 0), (0, f_pad - d_ff))), g_pre], axis=1)
        g = w["conv_b"] + sum(w["conv_w"][j] * padded[:, j:j + s] for j in range(CONV_W))
        hid = (jax.nn.gelu(g, approximate=False) * u).astype(BF16).reshape(n, f_pad)
        new_conv = padded[:, s:, :d_ff]
    down = _matmul(hid, w["w_down"], layer, F32, 256, 512)
    x2, x2b = _add_layernorm(x1, down, w["ln2_g"], w["ln2_b"], alpha)
    return x2, x2b, (dk, dv, new_mla, new_conv)


def kernel(x_prompt, x_sample, cache_diff_k, cache_diff_v, cache_mla, state_conv, page_table, w_in, q_norm, w_q_up, kv_norm, w_kv_up, lam_q1, lam_k1, lam_q2, lam_k2, subln, w_o, ln1_g, ln1_b, w_gate, w_up, conv_w, conv_b, w_down, ln2_g, ln2_b, rel_table):
    depth, d_model, in_cols = w_in.shape
    b_p, s_p, _ = x_prompt.shape
    b_s, s_s, _ = x_sample.shape
    n_pool, page = cache_diff_k.shape[1:3]
    d2 = cache_diff_v.shape[-1]
    q_lora, h_mla, qk_dim = w_q_up.shape[1:]
    dc = w_kv_up.shape[1]
    d_rope = cache_mla.shape[-1] - dc
    d_nope = qk_dim - d_rope
    d_v = w_kv_up.shape[-1] - d_nope
    h_diff = rel_table.shape[1]
    d_ff = w_gate.shape[-1]
    dims = (h_diff, d2, h_mla, d_nope, d_rope, dc, d_v, q_lora, d_ff)
    past_len = page_table.shape[1] * page
    t = ATTN_TILE
    assert page >= REL_MAX_DIST and t >= REL_MAX_DIST and page % LANE == 0 and s_s <= SUBLANE
    assert cache_diff_k.shape[3] == 1 and cache_diff_v.shape[3] == 1, "one shared key/value head"

    pos_p = jnp.arange(s_p, dtype=jnp.int32)
    pos_s = past_len + jnp.arange(s_s, dtype=jnp.int32)

    kk = jnp.arange(t, dtype=jnp.int32)[:, None]
    qq = jnp.arange(t, dtype=jnp.int32)[None, :]
    bias_p = _bias_from_dist(rel_table, jnp.stack([qq - kk, t + qq - kk]))
    tt = jnp.arange(SUBLANE, dtype=jnp.int32)[:, None]
    jj = jnp.arange(page, dtype=jnp.int32)[None, :]
    new_ok = jnp.logical_and(jj <= tt, jj < s_s)
    bias_s = _bias_from_dist(rel_table, jnp.stack([page + tt - jj, jnp.where(new_ok, tt - jj, -1)]))
    bias_s = bias_s[:, :, :s_s].transpose(0, 2, 1, 3).reshape(2, s_s * h_diff, page)
    bias_s = jnp.concatenate([bias_s, bias_s], axis=1)
    bias = {
        "prefill": bias_p,
        "dec_last": jnp.stack([jnp.zeros_like(bias_s[0]), bias_s[0]]),
        "dec_new": bias_s[1],
        "dec_mask": jnp.where(jnp.repeat(new_ok[:s_s], h_mla, axis=0), 0.0, NEG_INF).astype(F32),
    }

    cache_kt = cache_diff_k.transpose(0, 1, 3, 4, 5, 2).reshape(depth, n_pool, d2, page)
    cache_v = cache_diff_v.reshape(depth, n_pool, page, d2)
    cache_mt = cache_mla.transpose(0, 1, 3, 2)

    f_pad = -(-d_ff // FF_TILE) * FF_TILE
    pad_f = f_pad - d_ff
    big = {
        "w_in": jnp.pad(w_in, ((0, 0), (0, 0), (0, (-in_cols) % LANE))).astype(BF16),
        "w_q_up": jnp.concatenate([w_q_up[..., :d_nope].reshape(depth, q_lora, -1),
                                   w_q_up[..., d_nope:].reshape(depth, q_lora, -1)], axis=2).astype(BF16),
        "w_o": w_o.astype(BF16),
        "w_gate": jnp.pad(w_gate, ((0, 0), (0, 0), (0, pad_f))).astype(BF16),
        "w_up": jnp.pad(w_up, ((0, 0), (0, 0), (0, pad_f))).astype(BF16),
        "w_down": jnp.pad(w_down, ((0, 0), (0, pad_f), (0, 0))).astype(BF16),
    }
    y_p, y_s = x_prompt.reshape(b_p * s_p, d_model), x_sample.reshape(b_s * s_s, d_model)
    yb_p, yb_s = y_p.astype(BF16), y_s.astype(BF16)
    st_p, st_s = [], []
    for l in range(depth):
        lam_init = 0.8 - 0.6 * math.exp(-0.3 * l)
        lam = (jnp.exp(jnp.sum(lam_q1[l] * lam_k1[l])) - jnp.exp(jnp.sum(lam_q2[l] * lam_k2[l])) + lam_init)
        w = dict(big)
        w.update({
            "alpha": (2 * depth) ** 0.25,
            "mla_scale": float(qk_dim) ** -0.5,
            "lam": lam.reshape(1, 1).astype(F32),
            "lam_init": lam_init,
            "w_uk": w_kv_up[l][:, :, :d_nope].transpose(1, 0, 2).astype(BF16),
            "w_uv": w_kv_up[l][:, :, d_nope:].transpose(1, 0, 2).astype(BF16),
            "conv_w": jnp.pad(conv_w[l], ((0, 0), (0, pad_f))),
            "conv_b": jnp.pad(conv_b[l], (0, pad_f)).reshape(1, f_pad),
            "q_norm": q_norm[l], "kv_norm": kv_norm[l], "subln": subln[l],
            "ln1_g": ln1_g[l], "ln1_b": ln1_b[l], "ln2_g": ln2_g[l], "ln2_b": ln2_b[l],
        })
        y_p, yb_p, sp = _layer(y_p, yb_p, pos_p, None, l, w, bias, dims, (b_p, s_p))
        y_s, yb_s, ss = _layer(y_s, yb_s, pos_s, (page_table, cache_kt, cache_v, cache_mt, state_conv[l]),
                               l, w, bias, dims, (b_s, s_s))
        st_p.append(sp)
        st_s.append(ss)

    def stack(states, i, shape):
        return jnp.stack([st[i].reshape(shape) for st in states])

    outs = [y_p.reshape(x_prompt.shape), y_s.reshape(x_sample.shape)]
    for states, (bb, ss_) in ((st_p, (b_p, s_p)), (st_s, (b_s, s_s))):
        outs += [stack(states, 0, (bb, ss_, 1, 2, d2 // 2)), stack(states, 1, (bb, ss_, 1, d2)),
                 stack(states, 2, (bb, ss_, dc + d_rope)), stack(states, 3, (bb, CONV_W - 1, d_ff))]
    return tuple(outs)
```

```python
import functools
import math

import numpy as np
import jax
import jax.numpy as jnp
from jax import lax
from jax.experimental import pallas as pl
from jax.experimental.pallas import tpu as pltpu

F32 = jnp.float32
BF16 = jnp.bfloat16

ROPE_THETA = 10000.0
REL_MAX_DIST = 128
NORM_EPS = 1e-5
NEG_INF = -1e30
M_INIT = -1e37
CONV_W = 3

V7X_VMEM_LIMIT = 56 * 1024 * 1024
LANE = 128
SUBLANE = 8
ATTN_TILE = 256
HEAD_GROUP = 4
PAGES_PER_STEP = 16
FF_TILE = 512


def _cparams(n_axes):
    return pltpu.CompilerParams(dimension_semantics=("arbitrary",) * n_axes,
                                vmem_limit_bytes=V7X_VMEM_LIMIT)


def _pick(n, cands):
    for c in cands:
        if n % c == 0:
            return c
    return n


def _mm_kernel(x_ref, w_ref, o_ref):
    o_ref[...] = jnp.dot(x_ref[...], w_ref[...], preferred_element_type=F32).astype(o_ref.dtype)


def _mm_nt_kernel(x_ref, w_ref, o_ref):
    o_ref[...] = lax.dot_general(x_ref[...], w_ref[...], (((1,), (1,)), ((), ())),
                                 preferred_element_type=F32).astype(o_ref.dtype)


def _matmul(x, w, layer, out_dtype, tm, tn):
    m, k = x.shape
    n = w.shape[2]
    tm, tn = _pick(m, (tm, 256, 128)), _pick(n, (tn, 512, 256, 128))
    return pl.pallas_call(
        _mm_kernel,
        grid=(n // tn, m // tm),
        in_specs=[pl.BlockSpec((tm, k), lambda j, i: (i, 0)),
                  pl.BlockSpec((None, k, tn), lambda j, i: (layer, 0, j))],
        out_specs=pl.BlockSpec((tm, tn), lambda j, i: (i, j)),
        out_shape=jax.ShapeDtypeStruct((m, n), out_dtype),
        compiler_params=_cparams(2),
    )(x, w)


def _head_matmul(x, w, grid, x_spec, w_spec, o_spec, out_shape, body=_mm_kernel):
    return pl.pallas_call(
        body, grid=grid, in_specs=[x_spec, w_spec], out_specs=o_spec,
        out_shape=out_shape, compiler_params=_cparams(len(grid)),
    )(x, w)


def _latent_query_kernel(w_ref, x_ref, o_ref):
    n_heads, _, d_nope = w_ref.shape
    for h in range(n_heads):
        o_ref[0, h] = _dot_nt(w_ref[h], x_ref[0, :, h * d_nope:(h + 1) * d_nope]).astype(o_ref.dtype)


def _latent_queries_t(q_nope, w_uk, tm):
    b, s, _ = q_nope.shape
    h, dc, d_nope = w_uk.shape
    return pl.pallas_call(
        _latent_query_kernel,
        grid=(b, s // tm),
        in_specs=[pl.BlockSpec((h, dc, d_nope), lambda bi, i: (0, 0, 0)),
                  pl.BlockSpec((1, tm, h * d_nope), lambda bi, i: (bi, i, 0))],
        out_specs=pl.BlockSpec((1, h, dc, tm), lambda bi, i: (bi, 0, 0, i)),
        out_shape=jax.ShapeDtypeStruct((b, h, dc, s), BF16),
        compiler_params=_cparams(2),
    )(w_uk, q_nope)


def _latent_value_kernel(x_ref, w_ref, o_ref):
    n_heads, _, d_v = w_ref.shape
    for h in range(n_heads):
        o_ref[0, :, h * d_v:(h + 1) * d_v] = jnp.dot(x_ref[0, h], w_ref[h], preferred_element_type=F32)


def _latent_values(lat, w_uv, tm):
    b, h, s, dc = lat.shape
    d_v = w_uv.shape[2]
    return pl.pallas_call(
        _latent_value_kernel,
        grid=(b, s // tm),
        in_specs=[pl.BlockSpec((1, h, tm, dc), lambda bi, i: (bi, 0, i, 0)),
                  pl.BlockSpec((h, dc, d_v), lambda bi, i: (0, 0, 0))],
        out_specs=pl.BlockSpec((1, tm, h * d_v), lambda bi, i: (bi, i, 0)),
        out_shape=jax.ShapeDtypeStruct((b, s, h * d_v), F32),
        compiler_params=_cparams(2),
    )(lat, w_uv)


def _add_ln_kernel(x_ref, y_ref, g_ref, b_ref, o_ref, ob_ref, *, alpha):
    z = alpha * x_ref[...] + y_ref[...]
    zc = z - jnp.mean(z, axis=-1, keepdims=True)
    var = jnp.mean(zc * zc, axis=-1, keepdims=True)
    o = zc * lax.rsqrt(var + NORM_EPS) * g_ref[...] + b_ref[...]
    o_ref[...] = o
    ob_ref[...] = o.astype(ob_ref.dtype)


def _add_layernorm(x, y, g, b, alpha):
    m, d = x.shape
    tr = _pick(m, (256, 128))
    row = pl.BlockSpec((tr, d), lambda i: (i, 0))
    vec = pl.BlockSpec((1, d), lambda i: (0, 0))
    return pl.pallas_call(
        functools.partial(_add_ln_kernel, alpha=alpha),
        grid=(m // tr,),
        in_specs=[row, row, vec, vec],
        out_specs=[row, row],
        out_shape=[jax.ShapeDtypeStruct((m, d), F32), jax.ShapeDtypeStruct((m, d), BF16)],
        compiler_params=_cparams(1),
    )(x, y, g.reshape(1, d), b.reshape(1, d))


def _ffn_in_kernel(x_ref, wg_ref, wu_ref, cw_ref, cb_ref, prev_ref, h_ref, tail_ref, carry_ref, *,
                   tiles_per_seq):
    i = pl.program_id(1)
    x = x_ref[...]
    g = jnp.dot(x, wg_ref[...], preferred_element_type=F32)
    u = jnp.dot(x, wu_ref[...], preferred_element_type=F32)
    prev = jnp.where(i % tiles_per_seq == 0, prev_ref[0], carry_ref[...])
    row = lax.broadcasted_iota(jnp.int32, g.shape, 0)
    g1 = jnp.where(row == 0, prev[7:8], pltpu.roll(g, 1, 0))
    g2 = jnp.where(row == 0, prev[6:7], jnp.where(row == 1, prev[7:8], pltpu.roll(g, 2, 0)))
    cw = cw_ref[...]
    gc = cb_ref[...] + cw[0:1] * g2 + cw[1:2] * g1 + cw[2:3] * g
    act = 0.5 * gc * (1.0 + lax.erf(gc * np.float32(math.sqrt(0.5))))
    h_ref[...] = (act * u).astype(h_ref.dtype)
    tail = g[g.shape[0] - SUBLANE:]
    carry_ref[...] = tail
    tail_ref[0] = tail


def _ffn_in(x, w_gate, w_up, layer, conv_w, conv_b, prev, seq_len, tm):
    n, d = x.shape
    f = w_gate.shape[2]
    tn = FF_TILE
    assert seq_len % tm == 0 and f % tn == 0 and tm >= SUBLANE and CONV_W == 3
    tps = seq_len // tm
    col = lambda j, i: (0, j)
    wcol = pl.BlockSpec((None, d, tn), lambda j, i: (layer, 0, j))
    return pl.pallas_call(
        functools.partial(_ffn_in_kernel, tiles_per_seq=tps),
        grid=(f // tn, n // tm),
        in_specs=[pl.BlockSpec((tm, d), lambda j, i: (i, 0)), wcol, wcol,
                  pl.BlockSpec((CONV_W, tn), col), pl.BlockSpec((1, tn), col),
                  pl.BlockSpec((1, SUBLANE, tn), lambda j, i: (i // tps, 0, j))],
        out_specs=[pl.BlockSpec((tm, tn), lambda j, i: (i, j)),
                   pl.BlockSpec((1, SUBLANE, tn), lambda j, i: (i // tps, 0, j))],
        out_shape=[jax.ShapeDtypeStruct((n, f), BF16),
                   jax.ShapeDtypeStruct((n // seq_len, SUBLANE, f), F32)],
        scratch_shapes=[pltpu.VMEM((SUBLANE, tn), F32)],
        compiler_params=_cparams(2),
    )(x, w_gate, w_up, conv_w, conv_b, prev)


def _bucket_uppers(n_buckets):
    max_exact = n_buckets // 2
    n = np.arange(0, REL_MAX_DIST + 1)
    nf = np.maximum(n, 1).astype(np.float64)
    v = np.log(nf / max_exact) / math.log(REL_MAX_DIST / max_exact) * (n_buckets - max_exact)
    frac = np.abs(v - np.round(v))[(n > max_exact) & (n < REL_MAX_DIST)]
    assert frac.min() > 1e-3, "bucket boundary too close to an integer distance"
    large = np.minimum(max_exact + np.floor(np.maximum(v + 1e-9, 0)).astype(np.int64), n_buckets - 1)
    bucket = np.where(n < max_exact, n, large)
    assert bucket[REL_MAX_DIST] == n_buckets - 1
    return [int(n[bucket <= b].max()) for b in range(n_buckets - 1)]


def _bias_kernel(tab_ref, n_ref, o_ref, *, uppers):
    h = pl.program_id(1)
    n = n_ref[0]
    far = tab_ref[len(uppers), h]
    val = jnp.zeros(n.shape, F32)
    for b in range(len(uppers) - 1, -1, -1):
        val = jnp.where(n <= uppers[b], tab_ref[b, h] - far, val)
    o_ref[0, 0] = jnp.where(n < 0, NEG_INF, val)


def _bias_from_dist(rel_table, dist):
    g, r, c = dist.shape
    n_buckets, n_heads = rel_table.shape
    return pl.pallas_call(
        functools.partial(_bias_kernel, uppers=_bucket_uppers(n_buckets)),
        grid=(g, n_heads),
        in_specs=[pl.BlockSpec(memory_space=pltpu.SMEM),
                  pl.BlockSpec((1, r, c), lambda i, h: (i, 0, 0))],
        out_specs=pl.BlockSpec((1, 1, r, c), lambda i, h: (i, h, 0, 0)),
        out_shape=jax.ShapeDtypeStruct((g, n_heads, r, c), F32),
        compiler_params=_cparams(2),
    )(rel_table.astype(F32), dist)


_NT = (((1,), (1,)), ((), ()))


def _dot(a, b):
    return jnp.dot(a, b, preferred_element_type=F32)


def _dot_nt(a, b):
    return lax.dot_general(a, b, _NT, preferred_element_type=F32)


def _softmax_step(s, axis, pv, m_r, l_r, a_r, scale=None):
    m_old = m_r[...]
    m_new = jnp.maximum(m_old, jnp.max(s, axis=axis, keepdims=True))
    dm, ds = m_old - m_new, s - m_new
    if scale is not None:
        dm, ds = dm * scale, ds * scale
    alpha = jnp.exp(dm)
    p = jnp.exp(ds)
    l_r[...] = alpha * l_r[...] + jnp.sum(p, axis=axis, keepdims=True)
    a_r[...] = alpha * a_r[...] + pv(p.astype(BF16))
    m_r[...] = m_new


def _init_state(*state_refs):
    for m_ref, l_ref, acc_ref in zip(state_refs[0::3], state_refs[1::3], state_refs[2::3]):
        m_ref[...] = jnp.full(m_ref.shape, M_INIT, F32)
        l_ref[...] = jnp.zeros(l_ref.shape, F32)
        acc_ref[...] = jnp.zeros(acc_ref.shape, F32)


def _batched(fn, n):
    return lambda p: jnp.stack([fn(p[i]) for i in range(n)])


def _diff_prefill_kernel(lam_ref, q_ref, k_ref, vt_ref, bias_ref, o_ref, m_ref, l_ref, acc_ref):
    qi, ki = pl.program_id(1), pl.program_id(2)
    n_groups, n_tiles = m_ref.shape[:2]
    group = n_tiles // 2

    @pl.when(ki == 0)
    def _():
        _init_state(m_ref, l_ref, acc_ref)

    def sweep(near):
        k = k_ref[0]
        vt = vt_ref[0]

        def body(g, carry):
            tiles = []
            for u in range(group):
                h = g * group + u
                for j in range(2):
                    s = _dot(k, q_ref[0, h, j])
                    tiles.append(s + bias_ref[0, h] if near else s)
            _softmax_step(jnp.stack(tiles), 1, _batched(lambda p: _dot(vt, p), n_tiles),
                          m_ref.at[g], l_ref.at[g], acc_ref.at[g])
            return carry

        lax.fori_loop(0, n_groups, body, 0)

    pl.when(ki < qi - 1)(lambda: sweep(False))
    pl.when(jnp.logical_and(ki >= qi - 1, ki <= qi))(lambda: sweep(True))

    @pl.when(ki == qi)
    def _():
        lam = lam_ref[0, 0]

        def body(g, carry):
            o = acc_ref[g] / l_ref[g]
            for u in range(group):
                o_ref[0, g * group + u] = o[2 * u] - lam * o[2 * u + 1]
            return carry

        lax.fori_loop(0, n_groups, body, 0)


def _diff_prefill(qt, k, vt, bias, lam):
    b, h, _, d2, s = qt.shape
    t = bias.shape[-1]
    nq = s // t
    g = _pick(h, (HEAD_GROUP,))
    kblk = lambda bi, qi, ki: jnp.minimum(ki, qi)
    return pl.pallas_call(
        _diff_prefill_kernel,
        grid=(b, nq, nq),
        in_specs=[pl.BlockSpec(memory_space=pltpu.SMEM),
                  pl.BlockSpec((1, h, 2, d2, t), lambda bi, qi, ki: (bi, 0, 0, 0, qi)),
                  pl.BlockSpec((1, t, d2), lambda bi, qi, ki: (bi, kblk(bi, qi, ki), 0)),
                  pl.BlockSpec((1, d2, t), lambda bi, qi, ki: (bi, 0, kblk(bi, qi, ki))),
                  pl.BlockSpec((1, h, t, t), lambda bi, qi, ki: (jnp.clip(qi - ki, 0, 1), 0, 0, 0))],
        out_specs=pl.BlockSpec((1, h, d2, t), lambda bi, qi, ki: (bi, 0, 0, qi)),
        out_shape=jax.ShapeDtypeStruct((b, h, d2, s), F32),
        scratch_shapes=[pltpu.VMEM((h // g, 2 * g, 1, t), F32), pltpu.VMEM((h // g, 2 * g, 1, t), F32),
                        pltpu.VMEM((h // g, 2 * g, d2, t), F32)],
        compiler_params=_cparams(3),
    )(lam, qt, k, vt, bias)


def _mla_prefill_kernel(qlt_ref, qrt_ref, c_ref, r_ref, ct_ref, o_ref, m_ref, l_ref, acc_ref, *, scale):
    qi, ki = pl.program_id(1), pl.program_id(2)
    n_groups, group = m_ref.shape[:2]
    t = c_ref.shape[1]

    @pl.when(ki == 0)
    def _():
        _init_state(m_ref, l_ref, acc_ref)

    def sweep(diagonal):
        c, r, ct = c_ref[0], r_ref[0], ct_ref[0]
        if diagonal:
            hidden = (lax.broadcasted_iota(jnp.int32, (t, t), 0) > lax.broadcasted_iota(jnp.int32, (t, t), 1))

        def body(g, carry):
            tiles = []
            for u in range(group):
                h = g * group + u
                s = _dot(c, qlt_ref[0, h]) + _dot(r, qrt_ref[0, h])
                tiles.append(jnp.where(hidden, NEG_INF, s) if diagonal else s)
            _softmax_step(jnp.stack(tiles), 1, _batched(lambda p: _dot(ct, p), group),
                          m_ref.at[g], l_ref.at[g], acc_ref.at[g], scale)
            return carry

        lax.fori_loop(0, n_groups, body, 0)

    pl.when(ki < qi)(lambda: sweep(False))

    @pl.when(ki == qi)
    def _():
        sweep(True)

        def body(g, carry):
            o = (acc_ref[g] / l_ref[g]).astype(o_ref.dtype)
            for u in range(group):
                o_ref[0, g * group + u] = o[u]
            return carry

        lax.fori_loop(0, n_groups, body, 0)


def _mla_prefill(qlt, qrt, c, r, ct, scale, t):
    b, h, dc, s = qlt.shape
    dr = qrt.shape[2]
    nq = s // t
    g = _pick(h, (HEAD_GROUP,))
    q_map = lambda bi, qi, ki: (bi, 0, 0, qi)
    k_map = lambda bi, qi, ki: (bi, jnp.minimum(ki, qi), 0)
    return pl.pallas_call(
        functools.partial(_mla_prefill_kernel, scale=scale),
        grid=(b, nq, nq),
        in_specs=[pl.BlockSpec((1, h, dc, t), q_map), pl.BlockSpec((1, h, dr, t), q_map),
                  pl.BlockSpec((1, t, dc), k_map), pl.BlockSpec((1, t, dr), k_map),
                  pl.BlockSpec((1, dc, t), lambda bi, qi, ki: (bi, 0, jnp.minimum(ki, qi)))],
        out_specs=pl.BlockSpec((1, h, dc, t), q_map),
        out_shape=jax.ShapeDtypeStruct((b, h, dc, s), BF16),
        scratch_shapes=[pltpu.VMEM((h // g, g, 1, t), F32), pltpu.VMEM((h // g, g, 1, t), F32),
                        pltpu.VMEM((h // g, g, dc, t), F32)],
        compiler_params=_cparams(3),
    )(qlt, qrt, c, r, ct)


def _page_copies(pt_ref, caches, bufs, sems, layer, seq, chunk, slot):
    n_pages = bufs[0].shape[1]
    copies = []
    for j in range(n_pages):
        pg = pt_ref[seq, chunk * n_pages + j]
        for a, (cache, buf) in enumerate(zip(caches, bufs)):
            copies.append(pltpu.make_async_copy(cache.at[layer, pg], buf.at[slot, j], sems.at[slot, a]))
    return copies


def _decode_kernel(pt_ref, lam_ref, q2_ref, ql_ref, qr_ref, kt_hbm, v_hbm, mt_hbm,
                   ktn_ref, vn_ref, mtn_ref, bias_last_ref, bias_new_ref, mask_new_ref, od_ref, om_ref,
                   kt_buf, v_buf, mt_buf, sems, md_ref, ld_ref, ad_ref, mm_ref, lm_ref, am_ref, *,
                   layer, scale, dc):
    b, c = pl.program_id(0), pl.program_id(1)
    nb, nc = pl.num_programs(0), pl.num_programs(1)
    n_pages = kt_buf.shape[1]
    q2, ql, qr = q2_ref[0], ql_ref[0], qr_ref[0]
    diff_state = (md_ref, ld_ref, ad_ref)
    mla_state = (mm_ref, lm_ref, am_ref)
    copies = functools.partial(_page_copies, pt_ref, (kt_hbm, v_hbm, mt_hbm), (kt_buf, v_buf, mt_buf), sems, layer)

    step = b * nc + c
    slot = step % 2
    last = step == nb * nc - 1

    @pl.when(step == 0)
    def _():
        for cp in copies(b, c, slot):
            cp.start()

    @pl.when(jnp.logical_not(last))
    def _():
        wrap = c + 1 == nc
        for cp in copies(jnp.where(wrap, b + 1, b), jnp.where(wrap, 0, c + 1), 1 - slot):
            cp.start()

    @pl.when(c == 0)
    def _():
        _init_state(*diff_state, *mla_state)

    for cp in copies(b, c, slot):
        cp.wait()

    kt = jnp.concatenate([kt_buf[slot, j].astype(BF16) for j in range(n_pages)], axis=1)
    v = jnp.concatenate([v_buf[slot, j].astype(BF16) for j in range(n_pages)], axis=0)
    mt = jnp.concatenate([mt_buf[slot, j].astype(BF16) for j in range(n_pages)], axis=1)
    page = bias_last_ref.shape[-1]

    s_d = _dot(q2, kt)
    s_d = jnp.concatenate([s_d[:, :-page], s_d[:, -page:] + bias_last_ref[0]], axis=1)
    _softmax_step(s_d, 1, lambda p: _dot(p, v), *diff_state)
    ct = mt[:dc]
    s_m = _dot(ql, ct) + _dot(qr, mt[dc:])
    _softmax_step(s_m, 1, lambda p: _dot_nt(p, ct), *mla_state, scale)

    @pl.when(c == nc - 1)
    def _():
        vn = vn_ref[0]
        _softmax_step(_dot(q2, ktn_ref[0]) + bias_new_ref[...], 1, lambda p: _dot(p, vn), *diff_state)
        o = ad_ref[...] / ld_ref[...]
        half = o.shape[0] // 2
        od_ref[0] = o[:half] - lam_ref[0, 0] * o[half:]
        mtn = mtn_ref[0]
        ctn = mtn[:dc]
        s_n = _dot(ql, ctn) + _dot(qr, mtn[dc:]) + mask_new_ref[...]
        _softmax_step(s_n, 1, lambda p: _dot_nt(p, ctn), *mla_state, scale)
        om_ref[0] = (am_ref[...] / lm_ref[...]).astype(om_ref.dtype)


def _decode_attention(page_table, lam, q2, ql, qr, cache_kt, cache_v, cache_mt, layer,
                      kt_new, v_new, mt_new, bias_last, bias_new, mask_new, scale):
    b, r2, d2 = q2.shape
    r, dc = ql.shape[1:]
    dr = qr.shape[-1]
    n_tab = page_table.shape[1]
    page = cache_v.shape[2]
    p = PAGES_PER_STEP
    assert n_tab % p == 0
    nc = n_tab // p
    const = lambda bi, ci, pt: (0, 0)
    seq = lambda bi, ci, pt: (bi, 0, 0)
    hbm = pl.BlockSpec(memory_space=pl.ANY)

    grid_spec = pltpu.PrefetchScalarGridSpec(
        num_scalar_prefetch=1,
        grid=(b, nc),
        in_specs=[pl.BlockSpec(memory_space=pltpu.SMEM),
                  pl.BlockSpec((1, r2, d2), seq), pl.BlockSpec((1, r, dc), seq), pl.BlockSpec((1, r, dr), seq),
                  hbm, hbm, hbm,
                  pl.BlockSpec((1, d2, page), seq), pl.BlockSpec((1, page, d2), seq),
                  pl.BlockSpec((1, dc + dr, page), seq),
                  pl.BlockSpec((1, r2, page), lambda bi, ci, pt: (jnp.where(ci == nc - 1, 1, 0), 0, 0)),
                  pl.BlockSpec((r2, page), const), pl.BlockSpec((r, page), const)],
        out_specs=[pl.BlockSpec((1, r2 // 2, d2), seq), pl.BlockSpec((1, r, dc), seq)],
        scratch_shapes=[pltpu.VMEM((2, p, d2, page), F32), pltpu.VMEM((2, p, page, d2), F32),
                        pltpu.VMEM((2, p, dc + dr, page), F32), pltpu.SemaphoreType.DMA((2, 3)),
                        pltpu.VMEM((r2, 1), F32), pltpu.VMEM((r2, 1), F32), pltpu.VMEM((r2, d2), F32),
                        pltpu.VMEM((r, 1), F32), pltpu.VMEM((r, 1), F32), pltpu.VMEM((r, dc), F32)],
    )
    return pl.pallas_call(
        functools.partial(_decode_kernel, layer=layer, scale=scale, dc=dc),
        grid_spec=grid_spec,
        out_shape=[jax.ShapeDtypeStruct((b, r2 // 2, d2), F32), jax.ShapeDtypeStruct((b, r, dc), BF16)],
        compiler_params=_cparams(2),
    )(page_table, lam, q2, ql, qr, cache_kt, cache_v, cache_mt,
      kt_new, v_new, mt_new, bias_last, bias_new, mask_new)


def _rmsnorm(x, g):
    return x * lax.rsqrt(jnp.mean(x * x, axis=-1, keepdims=True) + NORM_EPS) * g


def _rope(x, pos):
    half = x.shape[-1] // 2
    inv = ROPE_THETA ** (-jnp.arange(half, dtype=F32) / half)
    ang = pos.astype(F32)[:, None] * inv
    ang = ang.reshape(ang.shape[:1] + (1,) * (x.ndim - 3) + (half,))
    cos, sin = jnp.cos(ang), jnp.sin(ang)
    x1, x2 = x[..., :half], x[..., half:]
    return jnp.concatenate([x1 * cos - x2 * sin, x1 * sin + x2 * cos], axis=-1)


def _half_masked(q, axis):
    d2 = q.shape[axis]
    idx = lax.broadcasted_iota(jnp.int32, q.shape, axis)
    zero = jnp.zeros_like(q)
    return jnp.stack([jnp.where(idx < d2 // 2, q, zero), jnp.where(idx >= d2 // 2, q, zero)], axis=axis)


def _layer(x, xb, pos, past, layer, w, bias, dims, shape):
    b, s = shape
    n, d = x.shape
    h_diff, d2, h_mla, d_nope, d_rope, dc, d_v, q_lora, d_ff = dims
    alpha, mla_scale = w["alpha"], w["mla_scale"]
    tm = _pick(n, (512, 256, 128))

    proj = _matmul(xb, w["w_in"], layer, F32, tm, _pick(w["w_in"].shape[2], (1152, 384, 128)))
    o = 0
    dq = proj[:, o:o + h_diff * d2]; o += h_diff * d2
    dk = proj[:, o:o + d2]; o += d2
    dv = proj[:, o:o + d2]; o += d2
    cq = proj[:, o:o + q_lora]; o += q_lora
    ckv = proj[:, o:o + dc]; o += dc
    kr = proj[:, o:o + d_rope]

    c_kv = _rmsnorm(ckv, w["kv_norm"])
    k_r = _rope(kr.reshape(b, s, d_rope), pos).reshape(n, d_rope)
    new_mla = jnp.concatenate([c_kv, k_r], axis=-1)

    q = _matmul(_rmsnorm(cq, w["q_norm"]).astype(BF16), w["w_q_up"], layer, F32, tm, 1024)
    q_nope = q[:, :h_mla * d_nope].astype(BF16)
    q_rope = _rope(q[:, h_mla * d_nope:].reshape(b, s, h_mla, d_rope), pos).astype(BF16)
    dq_b = (dq * (d2 // 2) ** -0.5).astype(BF16)
    dk_b = dk.astype(BF16).reshape(b, s, d2)
    dv_b = dv.astype(BF16).reshape(b, s, d2)
    new_mla_b = new_mla.astype(BF16).reshape(b, s, dc + d_rope)
    lam = w["lam"]

    if past is None:
        q_lat_t = _latent_queries_t(q_nope.reshape(b, s, h_mla * d_nope), w["w_uk"], tm)
        qt = _half_masked(dq_b.reshape(b, s, h_diff, d2).transpose(0, 2, 3, 1), 2)
        diff_o = _diff_prefill(qt, dk_b, dv_b.transpose(0, 2, 1), bias["prefill"], lam)
        diff_o = diff_o.transpose(0, 3, 1, 2)
        mla_lat = _mla_prefill(q_lat_t, q_rope.transpose(0, 2, 3, 1),
                               new_mla_b[..., :dc], new_mla_b[..., dc:],
                               new_mla_b[..., :dc].transpose(0, 2, 1), mla_scale, ATTN_TILE)
        mla_o = _latent_values(mla_lat.transpose(0, 1, 3, 2), w["w_uv"], tm).reshape(n, h_mla * d_v)
    else:
        page_table, cache_kt, cache_v, cache_mt, conv_prev = past
        page = cache_v.shape[2]
        q_lat = _head_matmul(
            q_nope, w["w_uk"], (h_mla, n // tm),
            pl.BlockSpec((tm, d_nope), lambda hi, i: (i, hi)),
            pl.BlockSpec((None, dc, d_nope), lambda hi, i: (hi, 0, 0)),
            pl.BlockSpec((tm, dc), lambda hi, i: (i, hi)),
            jax.ShapeDtypeStruct((n, h_mla * dc), BF16), body=_mm_nt_kernel)
        q2 = _half_masked(dq_b.reshape(b, s * h_diff, d2), 2).transpose(0, 2, 1, 3)
        q2 = q2.reshape(b, 2 * s * h_diff, d2)

        def pad_new(a):
            return jnp.pad(a, ((0, 0), (0, page - s), (0, 0)))

        diff_o, mla_lat = _decode_attention(
            page_table, lam, q2, q_lat.reshape(b, s * h_mla, dc), q_rope.reshape(b, s * h_mla, d_rope),
            cache_kt, cache_v, cache_mt, layer,
            pad_new(dk_b).transpose(0, 2, 1), pad_new(dv_b), pad_new(new_mla_b).transpose(0, 2, 1),
            bias["dec_last"], bias["dec_new"], bias["dec_mask"], mla_scale)
        diff_o = diff_o.reshape(b, s, h_diff, d2)
        mla_o = _head_matmul(
            mla_lat.reshape(n, h_mla * dc), w["w_uv"], (h_mla, n // tm),
            pl.BlockSpec((tm, dc), lambda hi, i: (i, hi)),
            pl.BlockSpec((None, dc, d_v), lambda hi, i: (hi, 0, 0)),
            pl.BlockSpec((tm, d_v), lambda hi, i: (i, hi)),
            jax.ShapeDtypeStruct((n, h_mla * d_v), F32))

    diff_o = _rmsnorm(diff_o, w["subln"]) * (1.0 - w["lam_init"])
    mix = jnp.concatenate([diff_o.reshape(n, h_diff * d2), mla_o], axis=-1).astype(BF16)
    attn = _matmul(mix, w["w_o"], layer, F32, tm, 1024)
    x1, x1b = _add_layernorm(x, attn, w["ln1_g"], w["ln1_b"], alpha)

    f_pad = w["w_gate"].shape[2]
    if past is None:
        prev = jnp.zeros((b, SUBLANE, f_pad), F32)
        hid, tail = _ffn_in(x1b, w["w_gate"], w["w_up"], layer, w["conv_w"], w["conv_b"], prev, s, tm)
        new_conv = tail[:, SUBLANE - (CONV_W - 1):, :d_ff]
    else:
        g_pre = _matmul(x1b, w["w_gate"], layer, F32, tm, FF_TILE).reshape(b, s, f_pad)
        u = _matmul(x1b, w["w_up"], layer, F32, tm, FF_TILE).reshape(b, s, f_pad)
        padded = jnp.concatenate([jnp.pad(conv_prev, ((0, 0), (0, 0), (0, f_pad - d_ff))), g_pre], axis=1)
        g = w["conv_b"] + sum(w["conv_w"][j] * padded[:, j:j + s] for j in range(CONV_W))
        hid = (jax.nn.gelu(g, approximate=False) * u).astype(BF16).reshape(n, f_pad)
        new_conv = padded[:, s:, :d_ff]
    down = _matmul(hid, w["w_down"], layer, F32, 256, 512)
    x2, x2b = _add_layernorm(x1, down, w["ln2_g"], w["ln2_b"], alpha)
    return x2, x2b, (dk, dv, new_mla, new_conv)


def kernel(x_prompt, x_sample, cache_diff_k, cache_diff_v, cache_mla, state_conv, page_table, w_in, q_norm, w_q_up, kv_norm, w_kv_up, lam_q1, lam_k1, lam_q2, lam_k2, subln, w_o, ln1_g, ln1_b, w_gate, w_up, conv_w, conv_b, w_down, ln2_g, ln2_b, rel_table):
    depth, d_model, in_cols = w_in.shape
    b_p, s_p, _ = x_prompt.shape
    b_s, s_s, _ = x_sample.shape
    n_pool, page = cache_diff_k.shape[1:3]
    d2 = cache_diff_v.shape[-1]
    q_lora, h_mla, qk_dim = w_q_up.shape[1:]
    dc = w_kv_up.shape[1]
    d_rope = cache_mla.shape[-1] - dc
    d_nope = qk_dim - d_rope
    d_v = w_kv_up.shape[-1] - d_nope
    h_diff = rel_table.shape[1]
    d_ff = w_gate.shape[-1]
    dims = (h_diff, d2, h_mla, d_nope, d_rope, dc, d_v, q_lora, d_ff)
    past_len = page_table.shape[1] * page
    t = ATTN_TILE
    assert page >= REL_MAX_DIST and t >= REL_MAX_DIST and page % LANE == 0 and s_s <= SUBLANE
    assert cache_diff_k.shape[3] == 1 and cache_diff_v.shape[3] == 1, "one shared key/value head"

    pos_p = jnp.arange(s_p, dtype=jnp.int32)
    pos_s = past_len + jnp.arange(s_s, dtype=jnp.int32)

    kk = jnp.arange(t, dtype=jnp.int32)[:, None]
    qq = jnp.arange(t, dtype=jnp.int32)[None, :]
    bias_p = _bias_from_dist(rel_table, jnp.stack([qq - kk, t + qq - kk]))
    tt = jnp.arange(SUBLANE, dtype=jnp.int32)[:, None]
    jj = jnp.arange(page, dtype=jnp.int32)[None, :]
    new_ok = jnp.logical_and(jj <= tt, jj < s_s)
    bias_s = _bias_from_dist(rel_table, jnp.stack([page + tt - jj, jnp.where(new_ok, tt - jj, -1)]))
    bias_s = bias_s[:, :, :s_s].transpose(0, 2, 1, 3).reshape(2, s_s * h_diff, page)
    bias_s = jnp.concatenate([bias_s, bias_s], axis=1)
    bias = {
        "prefill": bias_p,
        "dec_last": jnp.stack([jnp.zeros_like(bias_s[0]), bias_s[0]]),
        "dec_new": bias_s[1],
        "dec_mask": jnp.where(jnp.repeat(new_ok[:s_s], h_mla, axis=0), 0.0, NEG_INF).astype(F32),
    }

    cache_kt = cache_diff_k.transpose(0, 1, 3, 4, 5, 2).reshape(depth, n_pool, d2, page)
    cache_v = cache_diff_v.reshape(depth, n_pool, page, d2)
    cache_mt = cache_mla.transpose(0, 1, 3, 2)

    f_pad = -(-d_ff // FF_TILE) * FF_TILE
    pad_f = f_pad - d_ff
    big = {
        "w_in": jnp.pad(w_in, ((0, 0), (0, 0), (0, (-in_cols) % LANE))).astype(BF16),
        "w_q_up": jnp.concatenate([w_q_up[..., :d_nope].reshape(depth, q_lora, -1),
                                   w_q_up[..., d_nope:].reshape(depth, q_lora, -1)], axis=2).astype(BF16),
        "w_o": w_o.astype(BF16),
        "w_gate": jnp.pad(w_gate, ((0, 0), (0, 0), (0, pad_f))).astype(BF16),
        "w_up": jnp.pad(w_up, ((0, 0), (0, 0), (0, pad_f))).astype(BF16),
        "w_down": jnp.pad(w_down, ((0, 0), (0, pad_f), (0, 0))).astype(BF16),
    }
    y_p, y_s = x_prompt.reshape(b_p * s_p, d_model), x_sample.reshape(b_s * s_s, d_model)
    yb_p, yb_s = y_p.astype(BF16), y_s.astype(BF16)
    st_p, st_s = [], []
    for l in range(depth):
        lam_init = 0.8 - 0.6 * math.exp(-0.3 * l)
        lam = (jnp.exp(jnp.sum(lam_q1[l] * lam_k1[l])) - jnp.exp(jnp.sum(lam_q2[l] * lam_k2[l])) + lam_init)
        w = dict(big)
        w.update({
            "alpha": (2 * depth) ** 0.25,
            "mla_scale": float(qk_dim) ** -0.5,
            "lam": lam.reshape(1, 1).astype(F32),
            "lam_init": lam_init,
            "w_uk": w_kv_up[l][:, :, :d_nope].transpose(1, 0, 2).astype(BF16),
            "w_uv": w_kv_up[l][:, :, d_nope:].transpose(1, 0, 2).astype(BF16),
            "conv_w": jnp.pad(conv_w[l], ((0, 0), (0, pad_f))),
            "conv_b": jnp.pad(conv_b[l], (0, pad_f)).reshape(1, f_pad),
            "q_norm": q_norm[l], "kv_norm": kv_norm[l], "subln": subln[l],
            "ln1_g": ln1_g[l], "ln1_b": ln1_b[l], "ln2_g": ln2_g[l], "ln2_b": ln2_b[l],
        })
        y_p, yb_p, sp = _layer(y_p, yb_p, pos_p, None, l, w, bias, dims, (b_p, s_p))
        y_s, yb_s, ss = _layer(y_s, yb_s, pos_s, (page_table, cache_kt, cache_v, cache_mt, state_conv[l]),
                               l, w, bias, dims, (b_s, s_s))
        st_p.append(sp)
        st_s.append(ss)

    def stack(states, i, shape):
        return jnp.stack([st[i].reshape(shape) for st in states])

    outs = [y_p.reshape(x_prompt.shape), y_s.reshape(x_sample.shape)]
    for states, (bb, ss_) in ((st_p, (b_p, s_p)), (st_s, (b_s, s_s))):
        outs += [stack(states, 0, (bb, ss_, 1, 2, d2 // 2)), stack(states, 1, (bb, ss_, 1, d2)),
                 stack(states, 2, (bb, ss_, dc + d_rope)), stack(states, 3, (bb, CONV_W - 1, d_ff))]
    return tuple(outs)
```
